```python
import jax
import jax.numpy as jnp
from jax import lax
import numpy as np

D_MODEL = 4096
BATCH = 2
SEQ = 8192
DEPTH = 4

N_A_LAYERS = DEPTH // 2
N_B_LAYERS = DEPTH - N_A_LAYERS

RWKV_HEAD = 64
RWKV_HEADS = D_MODEL // RWKV_HEAD
DECAY_LORA = 128
AAA_LORA = 128
MV_LORA = 96
GATE_LORA = 480
GN_EPS = 64e-5
L2_EPS = 1e-12

ATT_HEAD = 128
ATT_Q_HEADS = 32
ATT_KV_HEADS = 8
BRANCHES = ((128, 1), (512, 4), (2048, 16))
N_BR = 3
BLOCK = 128
PAD_MULT = 2048
REL_BUCKETS = 32
REL_MAX_DIST = 2048

D_FF = 11008
CONV_W = 3
NORM_EPS = 1e-6

kernel_name = "rwkv7_yoco_dilated_attn_convffn"


def _rmsnorm(x, g, eps=NORM_EPS):
    x32 = x.astype(jnp.float32)
    y = x32 * lax.rsqrt(jnp.mean(x32 * x32, axis=-1, keepdims=True) + eps)
    return (y * g.astype(jnp.float32)).astype(x.dtype)


def _token_shift(x):
    return jnp.pad(x, ((0, 0), (1, 0), (0, 0)))[:, :-1]


def _conv_ffn(h, w_up, conv_w, conv_b, w_down):
    u = h @ w_up
    c = u.shape[-1]
    u = lax.conv_general_dilated(
        u, conv_w[:, None, :], window_strides=(1,), padding=[(CONV_W - 1, 0)],
        dimension_numbers=('NWC', 'WIO', 'NWC'), feature_group_count=c) + conv_b
    gate, val = jnp.split(u, 2, axis=-1)
    return (jax.nn.silu(gate) * val) @ w_down


def _rwkv7_scan(r, w, k, v, a, b):
    def step(state, inp):
        r_t, w_t, k_t, v_t, a_t, b_t = inp
        sa = jnp.einsum('bhvk,bhk->bhv', state, a_t)
        state = (state * w_t[:, :, None, :] + sa[..., None] * b_t[:, :, None, :]
                 + v_t[..., None] * k_t[:, :, None, :])
        return state, jnp.einsum('bhvk,bhk->bhv', state, r_t)
    xs = tuple(jnp.moveaxis(t.astype(jnp.float32), 1, 0) for t in (r, w, k, v, a, b))
    bsz, _, nh, n = r.shape
    s0 = jnp.zeros((bsz, nh, n, n), jnp.float32)
    _, ys = lax.scan(step, s0, xs)
    return jnp.moveaxis(ys, 0, 1)


def _rwkv7_time_mix(h, v_first, v_res, mu, w0, w1, w2, a0, a1, a2, g1, g2,
                    k_k, k_a, r_k, w_r, w_k, w_v, w_o, gn_w, gn_b):
    bsz, s, d = h.shape
    heads = lambda t: t.reshape(bsz, s, RWKV_HEADS, RWKV_HEAD)
    xx = _token_shift(h) - h
    xr, xw, xk, xv, xa, xg = [h + xx * mu[i] for i in range(6)]
    r = xr @ w_r
    w_log = -jax.nn.softplus(-(w0 + jnp.tanh(xw @ w1) @ w2)) - 0.5
    k = xk @ w_k
    v = xv @ w_v
    if v_res is None:
        v_first = v
    else:
        v0, v1, v2 = v_res
        v = v + (v_first - v) * jax.nn.sigmoid(v0 + (xv @ v1) @ v2)
    a = jax.nn.sigmoid(a0 + (xa @ a1) @ a2)
    g = jax.nn.sigmoid(xg @ g1) @ g2
    kk = heads(k * k_k).astype(jnp.float32)
    kk = kk / jnp.maximum(jnp.sqrt(jnp.sum(kk * kk, axis=-1, keepdims=True)), L2_EPS)
    k = k * (1.0 + (a - 1.0) * k_a)
    decay = jnp.exp(-jnp.exp(w_log.astype(jnp.float32)))
    a_h = heads(a).astype(jnp.float32)
    y = _rwkv7_scan(heads(r), heads(decay), heads(k), heads(v), -kk, kk * a_h)
    mean = jnp.mean(y, axis=-1, keepdims=True)
    var = jnp.mean(jnp.square(y - mean), axis=-1, keepdims=True)
    y = ((y - mean) * lax.rsqrt(var + GN_EPS)).reshape(bsz, s, d)
    y = (y * gn_w.astype(jnp.float32) + gn_b.astype(jnp.float32)).astype(h.dtype)
    bonus = jnp.sum(heads(r) * heads(k) * r_k, axis=-1, keepdims=True) * heads(v)
    out = (y + bonus.reshape(bsz, s, d)) * g
    return out @ w_o, v_first


def _t5_bucket(distance):
    max_exact = REL_BUCKETS // 2
    dist = np.asarray(distance, dtype=np.int64)
    scaled = np.log(np.maximum(dist, max_exact) / max_exact) / np.log(REL_MAX_DIST / max_exact)
    large = np.minimum(max_exact + (scaled * (REL_BUCKETS - max_exact)).astype(np.int64), REL_BUCKETS - 1)
    return np.where(dist < max_exact, dist, large).astype(np.int32)


def _shared_kv(x, norm_kv, w_kv, k_gain):
    bsz, s, _ = x.shape
    kv = (_rmsnorm(x, norm_kv) @ w_kv).reshape(bsz, s, N_BR, 2, ATT_KV_HEADS, ATT_HEAD)
    k = _rmsnorm(kv[:, :, :, 0], k_gain[:, None, :])
    v = kv[:, :, :, 1]
    return k, v


def _dilated_branch(q, k, v, rel_bias, window, dilation):
    band = window // dilation
    bsz, p, hq, c = q.shape
    g = k.shape[2]
    e = hq // g
    length = p // dilation
    nb = length // BLOCK

    def to_sub(t):
        t = jnp.swapaxes(t.reshape((bsz, length, dilation) + t.shape[2:]), 1, 2)
        return t.reshape((bsz, dilation, nb, BLOCK) + t.shape[3:])

    def from_sub(t):
        t = t.reshape((bsz, dilation, length) + t.shape[4:])
        return jnp.swapaxes(t, 1, 2).reshape((bsz, p) + t.shape[3:])

    def banded(t):
        prev = jnp.concatenate([jnp.zeros_like(t[:, :, :1]), t[:, :, :-1]], axis=2)
        return jnp.concatenate([prev, t], axis=3)

    qs = to_sub(q).reshape(bsz, dilation, nb, BLOCK, g, e, c)
    kb, vb = banded(to_sub(k)), banded(to_sub(v))
    s = jnp.einsum('brnqgec,brnkgc->brngeqk', qs, kb,
                   preferred_element_type=jnp.float32) * (c ** -0.5)
    qi = np.arange(BLOCK)[:, None]
    kj = np.arange(2 * BLOCK)[None, :]
    rel = qi + BLOCK - kj
    in_band = (rel >= 0) & (rel <= band)
    key_ok = ~((np.arange(nb)[:, None, None] == 0) & (kj[None] < BLOCK))
    mask = in_band[None] & key_ok
    bucket = _t5_bucket(np.clip(rel, 0, band) * dilation)
    bias = jnp.transpose(rel_bias[bucket], (2, 0, 1)).astype(jnp.float32)
    bias = bias.reshape(g, e, BLOCK, 2 * BLOCK)
    s = jnp.where(mask[None, None, :, None, None], s + bias, -jnp.inf)
    lse = jax.nn.logsumexp(s, axis=-1)
    prob = jnp.exp(s - lse[..., None]).astype(vb.dtype)
    o = jnp.einsum('brngeqk,brnkgc->brnqgec', prob, vb)
    o = from_sub(o.reshape(bsz, dilation, nb, BLOCK, hq, c))
    lse = from_sub(jnp.moveaxis(lse, -1, 3).reshape(bsz, dilation, nb, BLOCK, hq))
    return o, lse


def _dilated_attention(h, w_q, q_gain, k_sh, v_sh, rel_bias, w_o):
    bsz, s, _ = h.shape
    p = -(-s // PAD_MULT) * PAD_MULT
    q = (h @ w_q).reshape(bsz, s, N_BR, ATT_Q_HEADS, ATT_HEAD)
    q = _rmsnorm(q, q_gain[:, None, :])
    pad = lambda t: jnp.pad(t, ((0, 0), (0, p - s)) + ((0, 0),) * (t.ndim - 2))
    q, kp, vp = pad(q), pad(k_sh), pad(v_sh)
    outs, lses = [], []
    for i, (window, dilation) in enumerate(BRANCHES):
        o, l = _dilated_branch(q[:, :, i], kp[:, :, i], vp[:, :, i], rel_bias, window, dilation)
        outs.append(o[:, :s])
        lses.append(l[:, :s])
    alpha = jax.nn.softmax(jnp.stack(lses), axis=0)
    o = jnp.einsum('ibsh,ibshc->bshc', alpha.astype(outs[0].dtype), jnp.stack(outs))
    return o.reshape(bsz, s, ATT_Q_HEADS * ATT_HEAD) @ w_o


def setup_inputs(seed: int = 0) -> dict:
    key = jax.random.key(seed)
    ks = iter(jax.random.split(key, 48))
    d, f, na, nb = D_MODEL, D_FF, N_A_LAYERS, N_B_LAYERS

    def nrm(shape, scale):
        return jax.random.normal(next(ks), shape, jnp.float32) * scale

    def unif(shape, lo, hi):
        return jax.random.uniform(next(ks), shape, jnp.float32, lo, hi)

    def gain(shape):
        return 1.0 + nrm(shape, 0.02)

    return dict(
        x=nrm((BATCH, SEQ, d), 1.0),
        norm_mix=gain((DEPTH, d)),
        norm_ffn=gain((DEPTH, d)),
        rwkv_mu=unif((na, 6, d), 0.0, 1.0),
        rwkv_w0=unif((na, d), -6.0, -1.0),
        rwkv_w1=nrm((na, d, DECAY_LORA), d ** -0.5),
        rwkv_w2=nrm((na, DECAY_LORA, d), 0.5 * DECAY_LORA ** -0.5),
        rwkv_a0=nrm((na, d), 0.1),
        rwkv_a1=nrm((na, d, AAA_LORA), d ** -0.5),
        rwkv_a2=nrm((na, AAA_LORA, d), 0.5 * AAA_LORA ** -0.5),
        rwkv_v0=nrm((na - 1, d), 0.1),
        rwkv_v1=nrm((na - 1, d, MV_LORA), d ** -0.5),
        rwkv_v2=nrm((na - 1, MV_LORA, d), 0.5 * MV_LORA ** -0.5),
        rwkv_g1=nrm((na, d, GATE_LORA), d ** -0.5),
        rwkv_g2=nrm((na, GATE_LORA, d), GATE_LORA ** -0.5),
        rwkv_k_k=0.85 + nrm((na, d), 0.05),
        rwkv_k_a=1.0 + nrm((na, d), 0.05),
        rwkv_r_k=nrm((na, RWKV_HEADS, RWKV_HEAD), 0.1),
        rwkv_w_r=nrm((na, d, d), d ** -0.5),
        rwkv_w_k=nrm((na, d, d), d ** -0.5),
        rwkv_w_v=nrm((na, d, d), d ** -0.5),
        rwkv_w_o=nrm((na, d, d), d ** -0.5),
        rwkv_gn_w=gain((na, d)),
        rwkv_gn_b=nrm((na, d), 0.02),
        norm_kv=gain((d,)),
        attn_w_kv=nrm((d, N_BR * 2 * ATT_KV_HEADS * ATT_HEAD), d ** -0.5),
        attn_k_gain=gain((N_BR, ATT_HEAD)),
        attn_w_q=nrm((nb, d, N_BR * ATT_Q_HEADS * ATT_HEAD), d ** -0.5),
        attn_q_gain=gain((nb, N_BR, ATT_HEAD)),
        attn_w_o=nrm((nb, ATT_Q_HEADS * ATT_HEAD, d), (ATT_Q_HEADS * ATT_HEAD) ** -0.5),
        rel_bias=nrm((REL_BUCKETS, ATT_Q_HEADS), 0.2),
        ffn_w_up=nrm((DEPTH, d, 2 * f), d ** -0.5),
        ffn_conv_w=nrm((DEPTH, CONV_W, 2 * f), CONV_W ** -0.5),
        ffn_conv_b=nrm((DEPTH, 2 * f), 0.01),
        ffn_w_down=nrm((DEPTH, f, d), f ** -0.5),
    )


def reference(x, norm_mix, norm_ffn, rwkv_mu, rwkv_w0, rwkv_w1, rwkv_w2, rwkv_a0, rwkv_a1,
              rwkv_a2, rwkv_v0, rwkv_v1, rwkv_v2, rwkv_g1, rwkv_g2, rwkv_k_k, rwkv_k_a,
              rwkv_r_k, rwkv_w_r, rwkv_w_k, rwkv_w_v, rwkv_w_o, rwkv_gn_w, rwkv_gn_b,
              norm_kv, attn_w_kv, attn_k_gain, attn_w_q, attn_q_gain, attn_w_o, rel_bias,
              ffn_w_up, ffn_conv_w, ffn_conv_b, ffn_w_down):
    v_first = None
    k_sh = v_sh = None
    for layer in range(DEPTH):
        h = _rmsnorm(x, norm_mix[layer])
        if layer < N_A_LAYERS:
            v_res = None if layer == 0 else (rwkv_v0[layer - 1], rwkv_v1[layer - 1], rwkv_v2[layer - 1])
            mix, v_first = _rwkv7_time_mix(
                h, v_first, v_res, rwkv_mu[layer], rwkv_w0[layer], rwkv_w1[layer], rwkv_w2[layer],
                rwkv_a0[layer], rwkv_a1[layer], rwkv_a2[layer], rwkv_g1[layer], rwkv_g2[layer],
                rwkv_k_k[layer], rwkv_k_a[layer], rwkv_r_k[layer], rwkv_w_r[layer], rwkv_w_k[layer],
                rwkv_w_v[layer], rwkv_w_o[layer], rwkv_gn_w[layer], rwkv_gn_b[layer])
        else:
            if layer == N_A_LAYERS:
                k_sh, v_sh = _shared_kv(x, norm_kv, attn_w_kv, attn_k_gain)
            j = layer - N_A_LAYERS
            mix = _dilated_attention(h, attn_w_q[j], attn_q_gain[j], k_sh, v_sh, rel_bias, attn_w_o[j])
        x = x + mix
        x = x + _conv_ffn(_rmsnorm(x, norm_ffn[layer]), ffn_w_up[layer], ffn_conv_w[layer],
                          ffn_conv_b[layer], ffn_w_down[layer])
    return x
```

```python
import functools

import numpy as np
import jax
import jax.numpy as jnp
from jax import lax
from jax.experimental import pallas as pl
from jax.experimental.pallas import tpu as pltpu

F32 = jnp.float32
BF16 = jnp.bfloat16

RWKV_HEAD = 64
ATT_HEAD = 128
N_BR = 3
BRANCHES = ((128, 1), (512, 4), (2048, 16))
BLOCK = 128
REL_BUCKETS = 32
REL_MAX_DIST = 2048
CONV_W = 3
NORM_EPS = 1e-6
GN_EPS = 64e-5
L2_EPS = 1e-12

LANES = 128
SUBLANES = 8
VMEM_LIMIT_BYTES = 56 * 1024 * 1024

SCAN_CHUNK = 64


def _cparams(*sem):
    return pltpu.CompilerParams(dimension_semantics=sem, vmem_limit_bytes=VMEM_LIMIT_BYTES)


def _pick(n, pref):
    t = min(pref, n)
    while n % t:
        t //= 2
    return t


def _rms(x, g):
    return x * lax.rsqrt(jnp.mean(x * x, axis=-1, keepdims=True) + NORM_EPS) * g


def _dot(a, b):
    return jnp.dot(a, b, preferred_element_type=F32)


def _dot_nt(a, b):
    return lax.dot_general(a, b, (((1,), (1,)), ((), ())), preferred_element_type=F32)


def _dot_tn(a, b):
    return lax.dot_general(a, b, (((0,), (0,)), ((), ())), preferred_element_type=F32)


def _shift_rows(u, halo, n):
    rolled = pltpu.roll(u, n, 0)
    row = lax.broadcasted_iota(jnp.int32, u.shape, 0)
    for r in range(n):
        rolled = jnp.where(row == r, halo[SUBLANES - n + r:SUBLANES - n + r + 1, :], rolled)
    return rolled


def _rms_kernel(x_ref, g_ref, o_ref):
    o_ref[...] = _rms(x_ref[...], g_ref[...]).astype(o_ref.dtype)


def _rmsnorm(x, g, tm=256):
    t, d = x.shape
    tm = _pick(t, tm)
    return pl.pallas_call(
        _rms_kernel,
        grid=(t // tm,),
        in_specs=[pl.BlockSpec((tm, d), lambda i: (i, 0)), pl.BlockSpec((1, d), lambda i: (0, 0))],
        out_specs=pl.BlockSpec((tm, d), lambda i: (i, 0)),
        out_shape=jax.ShapeDtypeStruct((t, d), BF16),
        compiler_params=_cparams("parallel"),
        name="rmsnorm",
    )(x, g.reshape(1, d))


def _rms_mix_kernel(x_ref, halo_ref, g_ref, mu_ref, *o_refs, tiles_per_seq):
    i = pl.program_id(0)
    g = g_ref[...]
    h = _rms(x_ref[...], g)
    hh = _rms(halo_ref[...], g)
    hh = jnp.where(i % tiles_per_seq == 0, 0.0, hh)
    xx = _shift_rows(h, hh, 1) - h
    for n, o_ref in enumerate(o_refs):
        o_ref[...] = (h + xx * mu_ref[n:n + 1, :]).astype(o_ref.dtype)


def _rms_mix(x, g, mu, seq, tm=256):
    t, d = x.shape
    tm = _pick(seq, tm)
    nmix = mu.shape[0]
    hb = tm // SUBLANES
    return pl.pallas_call(
        functools.partial(_rms_mix_kernel, tiles_per_seq=seq // tm),
        grid=(t // tm,),
        in_specs=[
            pl.BlockSpec((tm, d), lambda i: (i, 0)),
            pl.BlockSpec((SUBLANES, d), lambda i: (jnp.maximum(i * hb - 1, 0), 0)),
            pl.BlockSpec((1, d), lambda i: (0, 0)),
            pl.BlockSpec((nmix, d), lambda i: (0, 0)),
        ],
        out_specs=[pl.BlockSpec((tm, d), lambda i: (i, 0))] * nmix,
        out_shape=[jax.ShapeDtypeStruct((t, d), BF16)] * nmix,
        compiler_params=_cparams("parallel"),
        name="rms_mix",
    )(x, x, g.reshape(1, d), mu)


def _headnorm(acc, gain, flag):
    outs = []
    for h in range(acc.shape[1] // ATT_HEAD):
        sl = slice(h * ATT_HEAD, (h + 1) * ATT_HEAD)
        y = acc[:, sl]
        outs.append(jnp.where(flag[:, sl] != 0.0, _rms(y, gain[:, sl]), y))
    return jnp.concatenate(outs, axis=1) if len(outs) > 1 else outs[0]


def _mm_kernel(a_ref, w_ref, *rest, epilogue):
    o_ref = rest[-1]
    acc = _dot(a_ref[...], w_ref[...])
    if epilogue == "tanh":
        acc = jnp.tanh(acc)
    elif epilogue == "sigmoid":
        acc = jax.nn.sigmoid(acc)
    elif epilogue == "residual":
        acc = rest[0][...] + acc
    elif epilogue == "headnorm":
        acc = _headnorm(acc, rest[0][...], rest[1][...])
    o_ref[...] = acc.astype(o_ref.dtype)


def _matmul(a, w, out_dtype, *, epilogue="none", extra=(), tm=1024, tn=256):
    m, k = a.shape
    n = w.shape[1]
    tm = _pick(m, tm)
    tn = n if n % LANES else _pick(n, tn)
    in_specs = [pl.BlockSpec((tm, k), lambda i, j: (i, 0)), pl.BlockSpec((k, tn), lambda i, j: (0, j))]
    for e in extra:
        if e.shape[0] == 1:
            in_specs.append(pl.BlockSpec((1, tn), lambda i, j: (0, j)))
        else:
            in_specs.append(pl.BlockSpec((tm, tn), lambda i, j: (i, j)))
    return pl.pallas_call(
        functools.partial(_mm_kernel, epilogue=epilogue),
        grid=(m // tm, n // tn),
        in_specs=in_specs,
        out_specs=pl.BlockSpec((tm, tn), lambda i, j: (i, j)),
        out_shape=jax.ShapeDtypeStruct((m, n), out_dtype),
        compiler_params=_cparams("parallel", "arbitrary"),
        name="matmul_" + epilogue,
    )(a, w, *extra)


def _ffn_up_kernel(h_ref, halo_ref, wg_ref, wv_ref, cwg_ref, cwv_ref, cbg_ref, cbv_ref, o_ref, *, tiles_per_seq):
    i = pl.program_id(0)
    first = i % tiles_per_seq == 0
    h = h_ref[...]
    halo = halo_ref[...]

    def conv(w_ref, cw_ref, cb_ref):
        w = w_ref[...]
        u = _dot(h, w)
        uh = jnp.where(first, 0.0, _dot(halo, w))
        cw = cw_ref[...]
        return (cw[2:3, :] * u + cw[1:2, :] * _shift_rows(u, uh, 1)
                + cw[0:1, :] * _shift_rows(u, uh, 2) + cb_ref[...])

    gate = conv(wg_ref, cwg_ref, cbg_ref)
    val = conv(wv_ref, cwv_ref, cbv_ref)
    o_ref[...] = (jax.nn.silu(gate) * val).astype(o_ref.dtype)


def _ffn_up(h, w_up, conv_w, conv_b, seq, tm=1024, tn=256):
    t, k = h.shape
    f = w_up.shape[1] // 2
    tm = _pick(seq, tm)
    tn = _pick(f, tn)
    nj = f // tn
    hb = tm // SUBLANES
    return pl.pallas_call(
        functools.partial(_ffn_up_kernel, tiles_per_seq=seq // tm),
        grid=(t // tm, nj),
        in_specs=[
            pl.BlockSpec((tm, k), lambda i, j: (i, 0)),
            pl.BlockSpec((SUBLANES, k), lambda i, j: (jnp.maximum(i * hb - 1, 0), 0)),
            pl.BlockSpec((k, tn), lambda i, j: (0, j)),
            pl.BlockSpec((k, tn), lambda i, j: (0, j + nj)),
            pl.BlockSpec((CONV_W, tn), lambda i, j: (0, j)),
            pl.BlockSpec((CONV_W, tn), lambda i, j: (0, j + nj)),
            pl.BlockSpec((1, tn), lambda i, j: (0, j)),
            pl.BlockSpec((1, tn), lambda i, j: (0, j + nj)),
        ],
        out_specs=pl.BlockSpec((tm, tn), lambda i, j: (i, j)),
        out_shape=jax.ShapeDtypeStruct((t, f), BF16),
        compiler_params=_cparams("parallel", "arbitrary"),
        name="ffn_up",
    )(h, h, w_up, w_up, conv_w, conv_w, conv_b.reshape(1, -1), conv_b.reshape(1, -1))


def _halfsum(x):
    outs = []
    for gidx in range(x.shape[1] // LANES):
        y = x[:, gidx * LANES:(gidx + 1) * LANES]
        lo = lax.broadcasted_iota(jnp.int32, y.shape, 1) < RWKV_HEAD
        s_lo = jnp.sum(jnp.where(lo, y, 0.0), axis=-1, keepdims=True)
        s_hi = jnp.sum(jnp.where(lo, 0.0, y), axis=-1, keepdims=True)
        outs.append(jnp.where(lo, s_lo, s_hi))
    return jnp.concatenate(outs, axis=1) if len(outs) > 1 else outs[0]


def _rwkv_prep_kernel(*refs, has_vres):
    if has_vres:
        (k_ref, v_ref, hw_ref, ha_ref, hg_ref, w2_ref, a2_ref, g2_ref, vec_ref,
         hv_ref, v2_ref, vf_ref, lw_ref, ko_ref, vo_ref, kk_ref, a_ref, g_ref) = refs
    else:
        (k_ref, v_ref, hw_ref, ha_ref, hg_ref, w2_ref, a2_ref, g2_ref, vec_ref,
         lw_ref, ko_ref, vo_ref, kk_ref, a_ref, g_ref) = refs
    vec = vec_ref[...]
    w0, a0, k_k, k_a, v0 = (vec[n:n + 1, :] for n in range(5))
    k = k_ref[...]
    v = v_ref[...]
    lw_ref[...] = -np.exp(-0.5).astype(np.float32) * jax.nn.sigmoid(w0 + _dot(hw_ref[...], w2_ref[...]))
    a = jax.nn.sigmoid(a0 + _dot(ha_ref[...], a2_ref[...]))
    g_ref[...] = _dot(hg_ref[...], g2_ref[...])
    if has_vres:
        v = v + (vf_ref[...] - v) * jax.nn.sigmoid(v0 + _dot(hv_ref[...], v2_ref[...]))
    kk = k * k_k
    kk = kk / jnp.maximum(jnp.sqrt(_halfsum(kk * kk)), L2_EPS)
    ko_ref[...] = k * (1.0 + (a - 1.0) * k_a)
    vo_ref[...] = v
    kk_ref[...] = kk
    a_ref[...] = a


def _rwkv_prep(k, v, hw, ha, hg, w2, a2, g2, vec, vres, tm=256, tn=512):
    t, d = k.shape
    tm = _pick(t, tm)
    tn = _pick(d, tn)
    tile = pl.BlockSpec((tm, tn), lambda i, j: (i, j))
    rows = lambda arr: pl.BlockSpec((tm, arr.shape[1]), lambda i, j: (i, 0))
    cols = lambda arr: pl.BlockSpec((arr.shape[0], tn), lambda i, j: (0, j))
    args = [k, v, hw, ha, hg, w2, a2, g2, vec]
    in_specs = [tile, tile, rows(hw), rows(ha), rows(hg), cols(w2), cols(a2), cols(g2), cols(vec)]
    if vres is not None:
        hv, v2, v_first = vres
        args += [hv, v2, v_first]
        in_specs += [rows(hv), cols(v2), tile]
    return pl.pallas_call(
        functools.partial(_rwkv_prep_kernel, has_vres=vres is not None),
        grid=(t // tm, d // tn),
        in_specs=in_specs,
        out_specs=[tile] * 6,
        out_shape=[jax.ShapeDtypeStruct((t, d), F32)] * 6,
        compiler_params=_cparams("parallel", "parallel"),
        name="rwkv_prep",
    )(*args)


def _scan_kernel(r_ref, lw_ref, k_ref, v_ref, kk_ref, a_ref, g_ref, gnw_ref, gnb_ref, rk_ref,
                 o_ref, s_ref, *, chunks):
    c = SCAN_CHUNK
    hd = RWKV_HEAD

    @pl.when(pl.program_id(2) == 0)
    def _():
        s_ref[...] = jnp.zeros_like(s_ref)

    row = lax.broadcasted_iota(jnp.int32, (c, c), 0)
    col = lax.broadcasted_iota(jnp.int32, (c, c), 1)
    strict = row > col
    incl = row >= col
    eye = (row == col).astype(F32)
    tri = incl.astype(BF16)
    lane = lax.broadcasted_iota(jnp.int32, (c, LANES), 1)
    head0 = lane < hd
    r2 = lax.broadcasted_iota(jnp.int32, (LANES, LANES), 0)
    c2 = lax.broadcasted_iota(jnp.int32, (LANES, LANES), 1)
    blockdiag = (r2 < hd) == (c2 < hd)
    gnw = gnw_ref[...]
    gnb = gnb_ref[...]
    rk = rk_ref[...]

    def split3(x):
        x0 = x.astype(BF16)
        e1 = x - x0.astype(F32)
        x1 = e1.astype(BF16)
        x2 = (e1 - x1.astype(F32)).astype(BF16)
        return x0, x1, x2

    def per_head(x):
        s0 = jnp.sum(jnp.where(head0, x, 0.0), axis=-1, keepdims=True)
        s1 = jnp.sum(jnp.where(head0, 0.0, x), axis=-1, keepdims=True)
        return jnp.where(head0, s0, s1)

    def body(j, carry):
        rows = pl.ds(pl.multiple_of(j * c, c), c)
        r = r_ref[rows, :]
        lw = lw_ref[rows, :]
        k = k_ref[rows, :]
        v = v_ref[rows, :]
        kk = kk_ref[rows, :]
        a = a_ref[rows, :]
        b = kk * a
        x0, x1, x2 = split3(lw)
        cum = _dot(tri, x0) + _dot(tri, x1) + _dot(tri, x2)
        p = jnp.exp(cum)
        pinv = jnp.exp(-cum)
        to_end = jnp.exp(cum[c - 1:c, :] - cum)
        at = -kk * jnp.exp(cum - lw)
        rt = r * p
        kt = (k * pinv).astype(BF16)
        bt = (b * pinv).astype(BF16)
        v16 = v.astype(BF16)
        s = s_ref[...]
        zs = _dot_nt(jnp.concatenate([at, rt], axis=0).astype(BF16), s.astype(BF16))
        z_heads, t_heads, rk_heads, rb_heads = [], [], [], []
        for h in range(2):
            mine = head0 if h == 0 else jnp.logical_not(head0)
            lhs = jnp.concatenate([jnp.where(mine, at, 0.0), jnp.where(mine, rt, 0.0)], axis=0).astype(BF16)
            sk = _dot_nt(lhs, kt)
            sb = _dot_nt(lhs, bt)
            ak = jnp.where(strict, sk[:c], 0.0)
            ab = jnp.where(strict, sb[:c], 0.0)
            rk_heads.append(jnp.where(incl, sk[c:], 0.0).astype(BF16))
            rb_heads.append(jnp.where(incl, sb[c:], 0.0).astype(BF16))
            t = eye + ab
            pw = ab
            for _ in range(int(np.log2(c)) - 1):
                pw16 = pw.astype(BF16)
                pw = _dot(pw16, pw16)
                t = t + _dot(t.astype(BF16), pw.astype(BF16))
            t_heads.append(t.astype(BF16))
            z_heads.append(_dot(ak.astype(BF16), v16))
        z = zs[:c] + jnp.where(head0, z_heads[0], z_heads[1])
        z16 = z.astype(BF16)
        u = jnp.where(head0, _dot(t_heads[0], z16), _dot(t_heads[1], z16))
        u16 = u.astype(BF16)
        y = zs[c:] + jnp.where(head0,
                               _dot(rk_heads[0], v16) + _dot(rb_heads[0], u16),
                               _dot(rk_heads[1], v16) + _dot(rb_heads[1], u16))
        vu = jnp.concatenate([v16, u16], axis=0)
        kb = jnp.concatenate([(k * to_end).astype(BF16), (b * to_end).astype(BF16)], axis=0)
        s_ref[...] = s * p[c - 1:c, :] + jnp.where(blockdiag, _dot_tn(vu, kb), 0.0)
        mean = per_head(y) * (1.0 / hd)
        yc = y - mean
        var = per_head(yc * yc) * (1.0 / hd)
        yn = yc * lax.rsqrt(var + GN_EPS) * gnw + gnb
        bonus = per_head(r * k * rk) * v
        o_ref[rows, :] = ((yn + bonus) * g_ref[rows, :]).astype(o_ref.dtype)
        return carry

    lax.fori_loop(0, chunks, body, 0)


def _rwkv_scan(r, lw, k, v, kk, a, g, gn_w, gn_b, r_k, batch, seq, tc=256):
    t, d = r.shape
    tc = _pick(seq, tc)
    steps = seq // tc
    tile = pl.BlockSpec((tc, LANES), lambda bi, pi, ci: (bi * steps + ci, pi))
    vec = pl.BlockSpec((1, LANES), lambda bi, pi, ci: (0, pi))
    return pl.pallas_call(
        functools.partial(_scan_kernel, chunks=tc // SCAN_CHUNK),
        grid=(batch, d // LANES, steps),
        in_specs=[tile] * 7 + [vec] * 3,
        out_specs=tile,
        out_shape=jax.ShapeDtypeStruct((t, d), BF16),
        scratch_shapes=[pltpu.VMEM((LANES, LANES), F32)],
        compiler_params=_cparams("parallel", "parallel", "arbitrary"),
        name="rwkv_scan",
    )(r, lw, k, v, kk, a, g, gn_w.reshape(1, d), gn_b.reshape(1, d), r_k.reshape(1, d))


def _attn_kernel(q_ref, kp_ref, kc_ref, vp_ref, vc_ref, bias_ref, o_ref, lse_ref, *, groups, per_group):
    n = pl.program_id(2)
    blk = BLOCK
    scale = ATT_HEAD ** -0.5
    qi = lax.broadcasted_iota(jnp.int32, (blk, 2 * blk), 0)
    kj = lax.broadcasted_iota(jnp.int32, (blk, 2 * blk), 1)
    rel = qi + blk - kj
    mask1 = (rel >= 0) & (rel <= blk) & ((n > 0) | (kj >= blk))
    mask = jnp.concatenate([mask1] * per_group, axis=0)
    for g in range(groups):
        ks = slice(g * ATT_HEAD, (g + 1) * ATT_HEAD)
        kmat = jnp.concatenate([kp_ref[0, :, ks], kc_ref[0, :, ks]], axis=0)
        vmat = jnp.concatenate([vp_ref[0, :, ks], vc_ref[0, :, ks]], axis=0)
        h0 = g * per_group
        q = jnp.concatenate([q_ref[0, :, (h0 + e) * ATT_HEAD:(h0 + e + 1) * ATT_HEAD]
                             for e in range(per_group)], axis=0)
        bias = bias_ref[h0:h0 + per_group].reshape(per_group * blk, 2 * blk)
        s = _dot_nt(q, kmat) * scale + bias
        s = jnp.where(mask, s, -jnp.inf)
        m = jnp.max(s, axis=-1, keepdims=True)
        p = jnp.exp(s - m)
        l = jnp.sum(p, axis=-1, keepdims=True)
        o = _dot(p.astype(BF16), vmat) / l
        lse = m + jnp.log(l)
        for e in range(per_group):
            hs = slice((h0 + e) * ATT_HEAD, (h0 + e + 1) * ATT_HEAD)
            rs = slice(e * blk, (e + 1) * blk)
            o_ref[0, :, hs] = o[rs].astype(o_ref.dtype)
            lse_ref[0, :, hs] = jnp.broadcast_to(lse[rs], (blk, ATT_HEAD))


def _attn_branch(q, kv, bias, branch, dilation, batch, seq, heads, groups):
    length = seq // dilation
    nb = length // BLOCK
    qw = heads * ATT_HEAD
    kw = groups * ATT_HEAD
    qv = q.reshape(batch, length, dilation * N_BR * qw)
    kvv = kv.reshape(batch, length, dilation * N_BR * 2 * kw)
    qspec = pl.BlockSpec((1, BLOCK, qw), lambda b, r, n: (b, n, r * N_BR + branch))
    kcol = lambda r, sel: (r * N_BR + branch) * 2 + sel
    cur = lambda sel: pl.BlockSpec((1, BLOCK, kw), lambda b, r, n: (b, n, kcol(r, sel)))
    prev = lambda sel: pl.BlockSpec((1, BLOCK, kw), lambda b, r, n: (b, jnp.maximum(n - 1, 0), kcol(r, sel)))
    ospec = pl.BlockSpec((1, BLOCK, qw), lambda b, r, n: (b, n, r))
    o, lse = pl.pallas_call(
        functools.partial(_attn_kernel, groups=groups, per_group=heads // groups),
        grid=(batch, dilation, nb),
        in_specs=[qspec, prev(0), cur(0), prev(1), cur(1),
                  pl.BlockSpec((heads, BLOCK, 2 * BLOCK), lambda b, r, n: (0, 0, 0))],
        out_specs=[ospec, ospec],
        out_shape=[jax.ShapeDtypeStruct((batch, length, dilation * qw), BF16),
                   jax.ShapeDtypeStruct((batch, length, dilation * qw), F32)],
        compiler_params=_cparams("parallel", "parallel", "arbitrary"),
        name=f"attn_d{dilation}",
    )(qv, kvv, kvv, kvv, kvv, bias)
    return o.reshape(batch * seq, qw), lse.reshape(batch * seq, qw)


def _combine_kernel(o0, o1, o2, l0, l1, l2, out_ref):
    ls = [l0[...], l1[...], l2[...]]
    m = jnp.maximum(jnp.maximum(ls[0], ls[1]), ls[2])
    es = [jnp.exp(x - m) for x in ls]
    num = es[0] * o0[...].astype(F32) + es[1] * o1[...].astype(F32) + es[2] * o2[...].astype(F32)
    out_ref[...] = (num / (es[0] + es[1] + es[2])).astype(out_ref.dtype)


def _combine(outs, lses, tm=512, tn=1024):
    t, w = outs[0].shape
    tm = _pick(t, tm)
    tn = _pick(w, tn)
    tile = pl.BlockSpec((tm, tn), lambda i, j: (i, j))
    return pl.pallas_call(
        _combine_kernel,
        grid=(t // tm, w // tn),
        in_specs=[tile] * 6,
        out_specs=tile,
        out_shape=jax.ShapeDtypeStruct((t, w), BF16),
        compiler_params=_cparams("parallel", "parallel"),
        name="attn_combine",
    )(*outs, *lses)


def _t5_bucket(distance):
    max_exact = REL_BUCKETS // 2
    dist = np.asarray(distance, dtype=np.int64)
    scaled = np.log(np.maximum(dist, max_exact) / max_exact) / np.log(REL_MAX_DIST / max_exact)
    large = np.minimum(max_exact + (scaled * (REL_BUCKETS - max_exact)).astype(np.int64), REL_BUCKETS - 1)
    return np.where(dist < max_exact, dist, large).astype(np.int32)


def _bias_table(rel_bias, window, dilation):
    band = window // dilation
    qi = np.arange(BLOCK)[:, None]
    kj = np.arange(2 * BLOCK)[None, :]
    rel = qi + BLOCK - kj
    bucket = _t5_bucket(np.clip(rel, 0, band) * dilation)
    return jnp.transpose(rel_bias[bucket], (2, 0, 1)).astype(F32)


def _conv_ffn_layer(x, norm_g, w_up, conv_w, conv_b, w_down, seq):
    h = _rmsnorm(x, norm_g)
    act = _ffn_up(h, w_up.astype(BF16), conv_w, conv_b, seq)
    return _matmul(act, w_down.astype(BF16), F32, epilogue="residual", extra=(x,), tm=512)


def _rwkv_layer(x, v_first, vres_w, norm_g, mu, w0, w1, w2, a0, a1, a2, g1, g2, k_k, k_a, r_k,
                w_r, w_k, w_v, w_o, gn_w, gn_b, batch, seq):
    d = x.shape[1]
    xr, xw, xk, xv, xa, xg = _rms_mix(x, norm_g, mu, seq)
    r = _matmul(xr, w_r.astype(BF16), F32)
    k = _matmul(xk, w_k.astype(BF16), F32)
    v = _matmul(xv, w_v.astype(BF16), F32)
    hw = _matmul(xw, w1.astype(BF16), BF16, epilogue="tanh")
    ha = _matmul(xa, a1.astype(BF16), BF16)
    hg = _matmul(xg, g1.astype(BF16), BF16, epilogue="sigmoid")
    zeros = jnp.zeros((d,), F32)
    if vres_w is None:
        vec = jnp.stack([w0, a0, k_k, k_a, zeros, zeros, zeros, zeros])
        vres = None
    else:
        v0, v1, v2 = vres_w
        vec = jnp.stack([w0, a0, k_k, k_a, v0, zeros, zeros, zeros])
        vres = (_matmul(xv, v1.astype(BF16), BF16), v2.astype(BF16), v_first)
    lw, k2, v2_, kk, a, g = _rwkv_prep(k, v, hw, ha, hg, w2.astype(BF16), a2.astype(BF16), g2.astype(BF16),
                                       vec, vres)
    if vres_w is None:
        v_first = v2_
    out = _rwkv_scan(r, lw, k2, v2_, kk, a, g, gn_w, gn_b, r_k, batch, seq)
    return _matmul(out, w_o.astype(BF16), F32, epilogue="residual", extra=(x,)), v_first


def _attn_layer(x, h, kv, w_q, q_gain, w_o, biases, batch, seq, heads, groups):
    gain = jnp.broadcast_to(q_gain[:, None, :], (N_BR, heads, ATT_HEAD)).reshape(1, -1)
    q = _matmul(h, w_q.astype(BF16), BF16, epilogue="headnorm", extra=(gain, jnp.ones_like(gain)), tn=1024)
    outs, lses = [], []
    for i, (_, dilation) in enumerate(BRANCHES):
        o, lse = _attn_branch(q, kv, biases[i], i, dilation, batch, seq, heads, groups)
        outs.append(o)
        lses.append(lse)
    o = _combine(outs, lses)
    return _matmul(o, w_o.astype(BF16), F32, epilogue="residual", extra=(x,))


def kernel(x, norm_mix, norm_ffn, rwkv_mu, rwkv_w0, rwkv_w1, rwkv_w2, rwkv_a0, rwkv_a1, rwkv_a2, rwkv_v0, rwkv_v1, rwkv_v2, rwkv_g1, rwkv_g2, rwkv_k_k, rwkv_k_a, rwkv_r_k, rwkv_w_r, rwkv_w_k, rwkv_w_v, rwkv_w_o, rwkv_gn_w, rwkv_gn_b, norm_kv, attn_w_kv, attn_k_gain, attn_w_q, attn_q_gain, attn_w_o, rel_bias, ffn_w_up, ffn_conv_w, ffn_conv_b, ffn_w_down):
    batch, seq, d = x.shape
    depth = norm_mix.shape[0]
    n_a = rwkv_mu.shape[0]
    heads = attn_w_o.shape[1] // ATT_HEAD
    groups = attn_w_kv.shape[1] // (N_BR * 2 * ATT_HEAD)
    assert seq % (BLOCK * BRANCHES[-1][1]) == 0 and d % LANES == 0
    x = x.reshape(batch * seq, d)
    v_first = None
    kv = None
    biases = None
    for layer in range(depth):
        if layer < n_a:
            vres_w = None if layer == 0 else (rwkv_v0[layer - 1], rwkv_v1[layer - 1], rwkv_v2[layer - 1])
            x, v_first = _rwkv_layer(
                x, v_first, vres_w, norm_mix[layer], rwkv_mu[layer], rwkv_w0[layer], rwkv_w1[layer],
                rwkv_w2[layer], rwkv_a0[layer], rwkv_a1[layer], rwkv_a2[layer], rwkv_g1[layer],
                rwkv_g2[layer], rwkv_k_k[layer], rwkv_k_a[layer], rwkv_r_k[layer], rwkv_w_r[layer],
                rwkv_w_k[layer], rwkv_w_v[layer], rwkv_w_o[layer], rwkv_gn_w[layer], rwkv_gn_b[layer],
                batch, seq)
        else:
            if layer == n_a:
                k_gain = jnp.broadcast_to(attn_k_gain[:, None, None, :], (N_BR, 2, groups, ATT_HEAD))
                k_flag = jnp.broadcast_to(jnp.array([1.0, 0.0], F32)[None, :, None, None],
                                          (N_BR, 2, groups, ATT_HEAD))
                kv = _matmul(_rmsnorm(x, norm_kv), attn_w_kv.astype(BF16), BF16, epilogue="headnorm",
                             extra=(k_gain.reshape(1, -1), k_flag.reshape(1, -1)), tn=1024)
                biases = [_bias_table(rel_bias, w, dl) for w, dl in BRANCHES]
            j = layer - n_a
            h = _rmsnorm(x, norm_mix[layer])
            x = _attn_layer(x, h, kv, attn_w_q[j], attn_q_gain[j], attn_w_o[j], biases, batch, seq, heads, groups)
        x = _conv_ffn_layer(x, norm_ffn[layer], ffn_w_up[layer], ffn_conv_w[layer], ffn_conv_b[layer],
                            ffn_w_down[layer], seq)
    return x.reshape(batch, seq, d)
```

```python
import functools

import numpy as np
import jax
import jax.numpy as jnp
from jax import lax
from jax.experimental import pallas as pl
from jax.experimental.pallas import tpu as pltpu

F32 = jnp.float32
BF16 = jnp.bfloat16

RWKV_HEAD = 64
ATT_HEAD = 128
N_BR = 3
BRANCHES = ((128, 1), (512, 4), (2048, 16))
BLOCK = 128
REL_BUCKETS = 32
REL_MAX_DIST = 2048
CONV_W = 3
NORM_EPS = 1e-6
GN_EPS = 64e-5
L2_EPS = 1e-12

LANES = 128
SUBLANES = 8
VMEM_LIMIT_BYTES = 56 * 1024 * 1024

SCAN_CHUNK = 64


def _cparams(*sem):
    return pltpu.CompilerParams(dimension_semantics=sem, vmem_limit_bytes=VMEM_LIMIT_BYTES)


def _pick(n, pref):
    t = min(pref, n)
    while n % t:
        t //= 2
    return t


def _rms(x, g):
    return x * lax.rsqrt(jnp.mean(x * x, axis=-1, keepdims=True) + NORM_EPS) * g


def _dot(a, b):
    return jnp.dot(a, b, preferred_element_type=F32)


def _dot_nt(a, b):
    return lax.dot_general(a, b, (((1,), (1,)), ((), ())), preferred_element_type=F32)


def _dot_tn(a, b):
    return lax.dot_general(a, b, (((0,), (0,)), ((), ())), preferred_element_type=F32)


def _shift_rows(u, halo, n):
    rolled = pltpu.roll(u, n, 0)
    row = lax.broadcasted_iota(jnp.int32, u.shape, 0)
    for r in range(n):
        rolled = jnp.where(row == r, halo[SUBLANES - n + r:SUBLANES - n + r + 1, :], rolled)
    return rolled


def _rms_kernel(x_ref, g_ref, o_ref):
    o_ref[...] = _rms(x_ref[...], g_ref[...]).astype(o_ref.dtype)


def _rmsnorm(x, g, tm=256):
    t, d = x.shape
    tm = _pick(t, tm)
    return pl.pallas_call(
        _rms_kernel,
        grid=(t // tm,),
        in_specs=[pl.BlockSpec((tm, d), lambda i: (i, 0)), pl.BlockSpec((1, d), lambda i: (0, 0))],
        out_specs=pl.BlockSpec((tm, d), lambda i: (i, 0)),
        out_shape=jax.ShapeDtypeStruct((t, d), BF16),
        compiler_params=_cparams("parallel"),
        name="rmsnorm",
    )(x, g.reshape(1, d))


def _rms_mix_kernel(x_ref, halo_ref, g_ref, mu_ref, *o_refs, tiles_per_seq):
    i = pl.program_id(0)
    g = g_ref[...]
    h = _rms(x_ref[...], g)
    hh = _rms(halo_ref[...], g)
    hh = jnp.where(i % tiles_per_seq == 0, 0.0, hh)
    xx = _shift_rows(h, hh, 1) - h
    for n, o_ref in enumerate(o_refs):
        o_ref[...] = (h + xx * mu_ref[n:n + 1, :]).astype(o_ref.dtype)


def _rms_mix(x, g, mu, seq, tm=256):
    t, d = x.shape
    tm = _pick(seq, tm)
    nmix = mu.shape[0]
    hb = tm // SUBLANES
    return pl.pallas_call(
        functools.partial(_rms_mix_kernel, tiles_per_seq=seq // tm),
        grid=(t // tm,),
        in_specs=[
            pl.BlockSpec((tm, d), lambda i: (i, 0)),
            pl.BlockSpec((SUBLANES, d), lambda i: (jnp.maximum(i * hb - 1, 0), 0)),
            pl.BlockSpec((1, d), lambda i: (0, 0)),
            pl.BlockSpec((nmix, d), lambda i: (0, 0)),
        ],
        out_specs=[pl.BlockSpec((tm, d), lambda i: (i, 0))] * nmix,
        out_shape=[jax.ShapeDtypeStruct((t, d), BF16)] * nmix,
        compiler_params=_cparams("parallel"),
        name="rms_mix",
    )(x, x, g.reshape(1, d), mu)


def _headnorm(acc, gain, flag):
    outs = []
    for h in range(acc.shape[1] // ATT_HEAD):
        sl = slice(h * ATT_HEAD, (h + 1) * ATT_HEAD)
        y = acc[:, sl]
        outs.append(jnp.where(flag[:, sl] != 0.0, _rms(y, gain[:, sl]), y))
    return jnp.concatenate(outs, axis=1) if len(outs) > 1 else outs[0]


def _mm_kernel(a_ref, w_ref, *rest, epilogue):
    o_ref = rest[-1]
    acc = _dot(a_ref[...], w_ref[...])
    if epilogue == "tanh":
        acc = jnp.tanh(acc)
    elif epilogue == "sigmoid":
        acc = jax.nn.sigmoid(acc)
    elif epilogue == "residual":
        acc = rest[0][...] + acc
    elif epilogue == "headnorm":
        acc = _headnorm(acc, rest[0][...], rest[1][...])
    o_ref[...] = acc.astype(o_ref.dtype)


def _matmul(a, w, out_dtype, *, epilogue="none", extra=(), tm=1024, tn=256):
    m, k = a.shape
    n = w.shape[1]
    tm = _pick(m, tm)
    tn = n if n % LANES else _pick(n, tn)
    in_specs = [pl.BlockSpec((tm, k), lambda i, j: (i, 0)), pl.BlockSpec((k, tn), lambda i, j: (0, j))]
    for e in extra:
        if e.shape[0] == 1:
            in_specs.append(pl.BlockSpec((1, tn), lambda i, j: (0, j)))
        else:
            in_specs.append(pl.BlockSpec((tm, tn), lambda i, j: (i, j)))
    return pl.pallas_call(
        functools.partial(_mm_kernel, epilogue=epilogue),
        grid=(m // tm, n // tn),
        in_specs=in_specs,
        out_specs=pl.BlockSpec((tm, tn), lambda i, j: (i, j)),
        out_shape=jax.ShapeDtypeStruct((m, n), out_dtype),
        compiler_params=_cparams("parallel", "arbitrary"),
        name="matmul_" + epilogue,
    )(a, w, *extra)


def _ffn_up_kernel(h_ref, halo_ref, wg_ref, wv_ref, cwg_ref, cwv_ref, cbg_ref, cbv_ref, o_ref, *, tiles_per_seq):
    i = pl.program_id(0)
    first = i % tiles_per_seq == 0
    h = h_ref[...]
    halo = halo_ref[...]

    def conv(w_ref, cw_ref, cb_ref):
        w = w_ref[...]
        u = _dot(h, w)
        uh = jnp.where(first, 0.0, _dot(halo, w))
        cw = cw_ref[...]
        return (cw[2:3, :] * u + cw[1:2, :] * _shift_rows(u, uh, 1)
                + cw[0:1, :] * _shift_rows(u, uh, 2) + cb_ref[...])

    gate = conv(wg_ref, cwg_ref, cbg_ref)
    val = conv(wv_ref, cwv_ref, cbv_ref)
    o_ref[...] = (jax.nn.silu(gate) * val).astype(o_ref.dtype)


def _ffn_up(h, w_up, conv_w, conv_b, seq, tm=1024, tn=256):
    t, k = h.shape
    f = w_up.shape[1] // 2
    tm = _pick(seq, tm)
    tn = _pick(f, tn)
    nj = f // tn
    hb = tm // SUBLANES
    return pl.pallas_call(
        functools.partial(_ffn_up_kernel, tiles_per_seq=seq // tm),
        grid=(t // tm, nj),
        in_specs=[
            pl.BlockSpec((tm, k), lambda i, j: (i, 0)),
            pl.BlockSpec((SUBLANES, k), lambda i, j: (jnp.maximum(i * hb - 1, 0), 0)),
            pl.BlockSpec((k, tn), lambda i, j: (0, j)),
            pl.BlockSpec((k, tn), lambda i, j: (0, j + nj)),
            pl.BlockSpec((CONV_W, tn), lambda i, j: (0, j)),
            pl.BlockSpec((CONV_W, tn), lambda i, j: (0, j + nj)),
            pl.BlockSpec((1, tn), lambda i, j: (0, j)),
            pl.BlockSpec((1, tn), lambda i, j: (0, j + nj)),
        ],
        out_specs=pl.BlockSpec((tm, tn), lambda i, j: (i, j)),
        out_shape=jax.ShapeDtypeStruct((t, f), BF16),
        compiler_params=_cparams("parallel", "arbitrary"),
        name="ffn_up",
    )(h, h, w_up, w_up, conv_w, conv_w, conv_b.reshape(1, -1), conv_b.reshape(1, -1))


def _halfsum(x):
    outs = []
    for gidx in range(x.shape[1] // LANES):
        y = x[:, gidx * LANES:(gidx + 1) * LANES]
        lo = lax.broadcasted_iota(jnp.int32, y.shape, 1) < RWKV_HEAD
        s_lo = jnp.sum(jnp.where(lo, y, 0.0), axis=-1, keepdims=True)
        s_hi = jnp.sum(jnp.where(lo, 0.0, y), axis=-1, keepdims=True)
        outs.append(jnp.where(lo, s_lo, s_hi))
    return jnp.concatenate(outs, axis=1) if len(outs) > 1 else outs[0]


def _rwkv_prep_kernel(*refs, has_vres):
    if has_vres:
        (k_ref, v_ref, hw_ref, ha_ref, hg_ref, w2_ref, a2_ref, g2_ref, vec_ref,
         hv_ref, v2_ref, vf_ref, lw_ref, ko_ref, vo_ref, kk_ref, a_ref, g_ref) = refs
    else:
        (k_ref, v_ref, hw_ref, ha_ref, hg_ref, w2_ref, a2_ref, g2_ref, vec_ref,
         lw_ref, ko_ref, vo_ref, kk_ref, a_ref, g_ref) = refs
    vec = vec_ref[...]
    w0, a0, k_k, k_a, v0 = (vec[n:n + 1, :] for n in range(5))
    k = k_ref[...]
    v = v_ref[...]
    lw_ref[...] = -np.exp(-0.5).astype(np.float32) * jax.nn.sigmoid(w0 + _dot(hw_ref[...], w2_ref[...]))
    a = jax.nn.sigmoid(a0 + _dot(ha_ref[...], a2_ref[...]))
    g_ref[...] = _dot(hg_ref[...], g2_ref[...])
    if has_vres:
        v = v + (vf_ref[...] - v) * jax.nn.sigmoid(v0 + _dot(hv_ref[...], v2_ref[...]))
    kk = k * k_k
    kk = kk / jnp.maximum(jnp.sqrt(_halfsum(kk * kk)), L2_EPS)
    ko_ref[...] = k * (1.0 + (a - 1.0) * k_a)
    vo_ref[...] = v
    kk_ref[...] = kk
    a_ref[...] = a


def _rwkv_prep(k, v, hw, ha, hg, w2, a2, g2, vec, vres, tm=256, tn=512):
    t, d = k.shape
    tm = _pick(t, tm)
    tn = _pick(d, tn)
    tile = pl.BlockSpec((tm, tn), lambda i, j: (i, j))
    rows = lambda arr: pl.BlockSpec((tm, arr.shape[1]), lambda i, j: (i, 0))
    cols = lambda arr: pl.BlockSpec((arr.shape[0], tn), lambda i, j: (0, j))
    args = [k, v, hw, ha, hg, w2, a2, g2, vec]
    in_specs = [tile, tile, rows(hw), rows(ha), rows(hg), cols(w2), cols(a2), cols(g2), cols(vec)]
    if vres is not None:
        hv, v2, v_first = vres
        args += [hv, v2, v_first]
        in_specs += [rows(hv), cols(v2), tile]
    return pl.pallas_call(
        functools.partial(_rwkv_prep_kernel, has_vres=vres is not None),
        grid=(t // tm, d // tn),
        in_specs=in_specs,
        out_specs=[tile] * 6,
        out_shape=[jax.ShapeDtypeStruct((t, d), F32)] * 6,
        compiler_params=_cparams("parallel", "parallel"),
        name="rwkv_prep",
    )(*args)


def _pair_masks():
    c = SCAN_CHUNK
    lane = lax.broadcasted_iota(jnp.int32, (c, LANES), 1)
    head0 = lane < RWKV_HEAD
    i = lax.broadcasted_iota(jnp.int32, (2 * c, 2 * c), 0)
    j = lax.broadcasted_iota(jnp.int32, (2 * c, 2 * c), 1)
    same = (i < c) == (j < c)
    ti = jnp.where(i < c, i, i - c)
    tj = jnp.where(j < c, j, j - c)
    row = lax.broadcasted_iota(jnp.int32, (c, c), 0)
    col = lax.broadcasted_iota(jnp.int32, (c, c), 1)
    return dict(head0=head0, strict=same & (ti > tj), incl=same & (ti >= tj), same=same,
                eye=(i == j).astype(F32), tri=(row >= col).astype(BF16))


def _split_heads(x, head0):
    return jnp.concatenate([jnp.where(head0, x, 0.0), jnp.where(head0, 0.0, x)], axis=0)


def _merge_heads(x, head0):
    c = x.shape[0] // 2
    return jnp.where(head0, x[:c], x[c:])


def _twice(x):
    return jnp.concatenate([x, x], axis=0)


def _cumsum_rows(x, tri):
    x0 = x.astype(BF16)
    e1 = x - x0.astype(F32)
    x1 = e1.astype(BF16)
    x2 = (e1 - x1.astype(F32)).astype(BF16)
    return _dot(tri, x0) + _dot(tri, x1) + _dot(tri, x2)


def _scan_inv_kernel(lw_ref, kk_ref, a_ref, t_ref, cum_ref, *, chunks, pairs):
    c = SCAN_CHUNK
    mk = _pair_masks()
    head0 = mk["head0"]
    units = [(slice(j * c, (j + 1) * c), slice(pi * LANES, (pi + 1) * LANES))
             for pi in range(pairs) for j in range(chunks)]
    nu = len(units)
    lw = [lw_ref[u] for u in units]
    kk = [kk_ref[u] for u in units]
    cum = [_cumsum_rows(x, mk["tri"]) for x in lw]
    for n, u in enumerate(units):
        cum_ref[u] = cum[n]
    at = [_split_heads(-kk[n] * jnp.exp(cum[n] - lw[n]), head0).astype(BF16) for n in range(nu)]
    bt = [_twice(kk[n] * a_ref[u] * jnp.exp(-cum[n])).astype(BF16) for n, u in enumerate(units)]
    ab = [jnp.where(mk["strict"], _dot_nt(at[n], bt[n]), 0.0) for n in range(nu)]
    t = [mk["eye"] + x for x in ab]
    pw = [x.astype(BF16) for x in ab]
    pw = [_dot(x, x).astype(BF16) for x in pw]
    doublings = int(np.log2(c)) - 1
    for it in range(doublings):
        t = [t[n] + _dot(t[n].astype(BF16), pw[n]) for n in range(nu)]
        if it < doublings - 1:
            pw = [_dot(x, x).astype(BF16) for x in pw]
    for n, u in enumerate(units):
        t_ref[u] = (t[n][:c] + t[n][c:]).astype(t_ref.dtype)


def _scan_kernel(r_ref, cum_ref, lw_ref, k_ref, v_ref, kk_ref, a_ref, g_ref, t_ref, gnw_ref, gnb_ref, rk_ref,
                 o_ref, s_ref, *, chunks, pairs):
    c = SCAN_CHUNK
    hd = RWKV_HEAD

    @pl.when(pl.program_id(2) == 0)
    def _():
        s_ref[...] = jnp.zeros_like(s_ref)

    mk = _pair_masks()
    head0 = mk["head0"]
    lanes = [slice(pi * LANES, (pi + 1) * LANES) for pi in range(pairs)]

    def per_head(x):
        s0 = jnp.sum(jnp.where(head0, x, 0.0), axis=-1, keepdims=True)
        s1 = jnp.sum(jnp.where(head0, 0.0, x), axis=-1, keepdims=True)
        return jnp.where(head0, s0, s1)

    def prepare(j):
        rows = slice(j * c, (j + 1) * c)
        pre = []
        for ls in lanes:
            u = (rows, ls)
            cum = cum_ref[u]
            k = k_ref[u]
            kk = kk_ref[u]
            b = kk * a_ref[u]
            at = -kk * jnp.exp(cum - lw_ref[u])
            rt = r_ref[u] * jnp.exp(cum)
            pinv = jnp.exp(-cum)
            to_end = jnp.exp(cum[c - 1:c, :] - cum)
            lhs = jnp.concatenate([_split_heads(at, head0), _split_heads(rt, head0)], axis=0).astype(BF16)
            pre.append(dict(
                u=u, lhs=lhs, v16=v_ref[u].astype(BF16), p_end=jnp.exp(cum[c - 1:c, :]),
                kt=_twice(k * pinv).astype(BF16), bt=_twice(b * pinv).astype(BF16),
                ar=jnp.concatenate([at, rt], axis=0).astype(BF16),
                kb_end=jnp.concatenate([(k * to_end).astype(BF16), (b * to_end).astype(BF16)], axis=0),
                tinv=_split_heads(t_ref[u].astype(F32), head0).astype(BF16)))
        return pre

    def scores_k(pre):
        for q in pre:
            q["sk"] = _dot_nt(q["lhs"], q["kt"])

    def scores_b(pre):
        for q in pre:
            sb = _dot_nt(q["lhs"][2 * c:], q["bt"])
            ak = jnp.where(mk["strict"], q["sk"][:2 * c], 0.0).astype(BF16)
            q["akv"] = _merge_heads(_dot(ak, _twice(q["v16"])), head0)
            q["rkb"] = jnp.concatenate([jnp.where(mk["incl"], q["sk"][2 * c:], 0.0),
                                        jnp.where(mk["incl"], sb, 0.0)], axis=1).astype(BF16)

    def finish(pre):
        for q in pre:
            u = q["u"]
            ls = u[1]
            vu2 = jnp.concatenate([_twice(q["v16"]), _twice(q["u16"])], axis=0)
            y = q["zs_r"] + _merge_heads(_dot(q["rkb"], vu2), head0)
            mean = per_head(y) * (1.0 / hd)
            yc = y - mean
            var = per_head(yc * yc) * (1.0 / hd)
            yn = yc * lax.rsqrt(var + GN_EPS) * gnw_ref[:, ls] + gnb_ref[:, ls]
            bonus = per_head(r_ref[u] * k_ref[u] * rk_ref[:, ls]) * v_ref[u]
            o_ref[u] = ((yn + bonus) * g_ref[u]).astype(o_ref.dtype)

    s = [s_ref[pi] for pi in range(pairs)]
    cur = prepare(0)
    scores_k(cur)
    scores_b(cur)
    prev = None
    for j in range(chunks):
        nxt = prepare(j + 1) if j + 1 < chunks else None
        zs = [_dot_nt(q["ar"], s[pi].astype(BF16)) for pi, q in enumerate(cur)]
        if nxt is not None:
            scores_k(nxt)
        for pi, q in enumerate(cur):
            z16 = (zs[pi][:c] + q["akv"]).astype(BF16)
            q["u16"] = _merge_heads(_dot(q["tinv"], _twice(z16)), head0).astype(BF16)
            q["zs_r"] = zs[pi][c:]
        if nxt is not None:
            scores_b(nxt)
        for pi, q in enumerate(cur):
            vu = jnp.concatenate([q["v16"], q["u16"]], axis=0)
            s[pi] = s[pi] * q["p_end"] + jnp.where(mk["same"], _dot_tn(vu, q["kb_end"]), 0.0)
        if prev is not None:
            finish(prev)
        prev, cur = cur, nxt
    finish(prev)
    for pi in range(pairs):
        s_ref[pi] = s[pi]


def _rwkv_scan(r, lw, k, v, kk, a, g, gn_w, gn_b, r_k, batch, seq, tc=256, pairs=4, tc_inv=256):
    t, d = r.shape
    tc = _pick(seq, tc)
    tc_inv = _pick(t, tc_inv)
    pairs = min(pairs, d // LANES)
    w = pairs * LANES
    itile = pl.BlockSpec((tc_inv, w), lambda i, pi: (i, pi))
    tinv, cum = pl.pallas_call(
        functools.partial(_scan_inv_kernel, chunks=tc_inv // SCAN_CHUNK, pairs=pairs),
        grid=(t // tc_inv, d // w),
        in_specs=[itile] * 3,
        out_specs=[itile] * 2,
        out_shape=[jax.ShapeDtypeStruct((t, d), BF16), jax.ShapeDtypeStruct((t, d), F32)],
        compiler_params=_cparams("parallel", "parallel"),
        name="rwkv_inv",
    )(lw, kk, a)
    steps = seq // tc
    tile = pl.BlockSpec((tc, w), lambda bi, pi, ci: (bi * steps + ci, pi))
    vec = pl.BlockSpec((1, w), lambda bi, pi, ci: (0, pi))
    return pl.pallas_call(
        functools.partial(_scan_kernel, chunks=tc // SCAN_CHUNK, pairs=pairs),
        grid=(batch, d // w, steps),
        in_specs=[tile] * 9 + [vec] * 3,
        out_specs=tile,
        out_shape=jax.ShapeDtypeStruct((t, d), BF16),
        scratch_shapes=[pltpu.VMEM((pairs, LANES, LANES), F32)],
        compiler_params=_cparams("parallel", "parallel", "arbitrary"),
        name="rwkv_scan",
    )(r, cum, lw, k, v, kk, a, g, tinv, gn_w.reshape(1, d), gn_b.reshape(1, d), r_k.reshape(1, d))


SPAN = BLOCK * BRANCHES[-1][1]
ATT_HPS = 2
UNITS_PER_ITER = 2


def _attn_kernel(*refs):
    nbr = len(BRANCHES)
    q_refs = [refs[i * ATT_HPS:(i + 1) * ATT_HPS] for i in range(nbr)]
    base = nbr * ATT_HPS
    kc_refs, vc_refs, kp_refs, vp_refs, bias_refs = (refs[base + j * nbr: base + (j + 1) * nbr] for j in range(5))
    o_ref, acc_ref, m_ref, l_ref, stage_ref = refs[base + 5 * nbr:]
    blk = BLOCK
    n = pl.program_id(1)
    scale = ATT_HEAD ** -0.5
    kj = lax.broadcasted_iota(jnp.int32, (ATT_HPS * blk, 2 * blk), 1)
    widest = nbr - 1
    d_max = BRANCHES[widest][1]
    sub = 4
    assert d_max == sub * sub and BRANCHES[1][1] == sub and BRANCHES[0][1] == 1
    run = SPAN // sub

    staged = [q_refs[widest][e] for e in range(ATT_HPS)] + [kc_refs[widest], vc_refs[widest],
                                                           kp_refs[widest], vp_refs[widest]]
    for a, src in enumerate(staged):
        for r4 in range(sub):
            stage_ref[a, r4 * run:(r4 + 1) * run, :] = src[pl.ds(r4, run, stride=sub), :]

    def load_unit(i, d, u):
        if i == widest:
            r_lo = u % sub
            off = r_lo * run + u // sub
            rows_s = pl.ds(off, blk, stride=sub)
            q = jnp.concatenate([stage_ref[e, rows_s, :] for e in range(ATT_HPS)], axis=0)
            kcat = jnp.concatenate([stage_ref[ATT_HPS + 2, rows_s, :], stage_ref[ATT_HPS, rows_s, :]], axis=0)
            vcat = jnp.concatenate([stage_ref[ATT_HPS + 3, rows_s, :], stage_ref[ATT_HPS + 1, rows_s, :]], axis=0)
            return q, kcat, vcat, pl.ds(u, blk, stride=d), True
        nb = u // d
        start = nb * (blk * d) + (u - nb * d)
        rows = pl.ds(start, blk, stride=d)
        q = jnp.concatenate([q_refs[i][e][rows, :] for e in range(ATT_HPS)], axis=0)
        prow_in_cur = pl.ds(jnp.maximum(start - blk * d, 0), blk, stride=d)
        prow_in_prev = pl.ds(u - nb * d, blk, stride=d)
        first = nb == 0
        kprev = jnp.where(first, kp_refs[i][prow_in_prev, :], kc_refs[i][prow_in_cur, :])
        vprev = jnp.where(first, vp_refs[i][prow_in_prev, :], vc_refs[i][prow_in_cur, :])
        kcat = jnp.concatenate([kprev, kc_refs[i][rows, :]], axis=0)
        vcat = jnp.concatenate([vprev, vc_refs[i][rows, :]], axis=0)
        return q, kcat, vcat, rows, first

    for order, i in enumerate(reversed(range(nbr))):
        d = BRANCHES[i][1]

        def body(it, carry, i=i, d=d, init=order == 0):
            loaded = [load_unit(i, d, it * UNITS_PER_ITER + x) for x in range(UNITS_PER_ITER)]
            bias = bias_refs[i][...].reshape(ATT_HPS * blk, 2 * blk)
            ss = []
            for q, kcat, _, _, first in loaded:
                s = _dot_nt((q * scale).astype(BF16), kcat.astype(BF16)) + bias
                ss.append(jnp.where((n == 0) & first & (kj < blk), -jnp.inf, s))
            ms = [jnp.max(s, axis=-1, keepdims=True) for s in ss]
            ps = [jnp.exp(s - m) for s, m in zip(ss, ms)]
            ls = [jnp.sum(p, axis=-1, keepdims=True) for p in ps]
            os_ = [_dot(p.astype(BF16), ld[2].astype(BF16)) for p, ld in zip(ps, loaded)]
            for x in range(UNITS_PER_ITER):
                rows = loaded[x][3]
                for e in range(ATT_HPS):
                    hr = slice(e * blk, (e + 1) * blk)
                    m_new = jnp.broadcast_to(ms[x][hr], (blk, ATT_HEAD))
                    l_new = jnp.broadcast_to(ls[x][hr], (blk, ATT_HEAD))
                    o_new = os_[x][hr]
                    if init:
                        m_ref[e, rows, :] = m_new
                        l_ref[e, rows, :] = l_new
                        acc_ref[e, rows, :] = o_new
                    else:
                        m_old = m_ref[e, rows, :]
                        m2 = jnp.maximum(m_old, m_new)
                        w_old = jnp.exp(m_old - m2)
                        w_new = jnp.exp(m_new - m2)
                        m_ref[e, rows, :] = m2
                        l_ref[e, rows, :] = l_ref[e, rows, :] * w_old + l_new * w_new
                        acc_ref[e, rows, :] = acc_ref[e, rows, :] * w_old + o_new * w_new
            return carry

        lax.fori_loop(0, (SPAN // blk) // UNITS_PER_ITER, body, 0)

    for e in range(ATT_HPS):
        o_ref[:, e * ATT_HEAD:(e + 1) * ATT_HEAD] = (acc_ref[e] / l_ref[e]).astype(o_ref.dtype)


def _attention(q, kv, biases, batch, seq, heads, groups):
    t = q.shape[0]
    nsb = seq // SPAN
    per_kv = heads // groups
    steps = heads // ATT_HPS
    in_specs, args = [], []
    for i in range(N_BR):
        for e in range(ATT_HPS):
            in_specs.append(pl.BlockSpec((SPAN, ATT_HEAD),
                                         lambda b, n, hp, i=i, e=e: (b * nsb + n, i * heads + hp * ATT_HPS + e)))
            args.append(q)
    kvcol = lambda i, sel, hp: (i * 2 + sel) * groups + (hp * ATT_HPS) // per_kv
    for sel in range(2):
        for i in range(N_BR):
            in_specs.append(pl.BlockSpec((SPAN, ATT_HEAD),
                                         lambda b, n, hp, i=i, sel=sel: (b * nsb + n, kvcol(i, sel, hp))))
            args.append(kv)
    for sel in range(2):
        for i, (_, d) in enumerate(BRANCHES):
            rows = BLOCK * d
            in_specs.append(pl.BlockSpec(
                (rows, ATT_HEAD),
                lambda b, n, hp, i=i, sel=sel, rows=rows: (jnp.maximum((b * nsb + n) * (SPAN // rows) - 1, 0),
                                                            kvcol(i, sel, hp))))
            args.append(kv)
    for i in range(N_BR):
        in_specs.append(pl.BlockSpec((ATT_HPS, BLOCK, 2 * BLOCK), lambda b, n, hp: (hp, 0, 0)))
        args.append(biases[i])
    return pl.pallas_call(
        _attn_kernel,
        grid=(batch, nsb, steps),
        in_specs=in_specs,
        out_specs=pl.BlockSpec((SPAN, ATT_HPS * ATT_HEAD), lambda b, n, hp: (b * nsb + n, hp)),
        out_shape=jax.ShapeDtypeStruct((t, heads * ATT_HEAD), BF16),
        scratch_shapes=[pltpu.VMEM((ATT_HPS, SPAN, ATT_HEAD), F32)] * 3
        + [pltpu.VMEM((ATT_HPS + 4, SPAN, ATT_HEAD), F32)],
        compiler_params=_cparams("parallel", "parallel", "arbitrary"),
        name="dilated_attn",
    )(*args)


def _t5_bucket(distance):
    max_exact = REL_BUCKETS // 2
    dist = np.asarray(distance, dtype=np.int64)
    scaled = np.log(np.maximum(dist, max_exact) / max_exact) / np.log(REL_MAX_DIST / max_exact)
    large = np.minimum(max_exact + (scaled * (REL_BUCKETS - max_exact)).astype(np.int64), REL_BUCKETS - 1)
    return np.where(dist < max_exact, dist, large).astype(np.int32)


def _bias_table(rel_bias, window, dilation):
    band = window // dilation
    qi = np.arange(BLOCK)[:, None]
    kj = np.arange(2 * BLOCK)[None, :]
    rel = qi + BLOCK - kj
    bucket = _t5_bucket(np.clip(rel, 0, band) * dilation)
    bias = jnp.transpose(rel_bias[bucket], (2, 0, 1)).astype(F32)
    return jnp.where(((rel >= 0) & (rel <= band))[None], bias, -jnp.inf)


def _conv_ffn_layer(x, norm_g, w_up, conv_w, conv_b, w_down, seq):
    h = _rmsnorm(x, norm_g)
    act = _ffn_up(h, w_up.astype(BF16), conv_w, conv_b, seq)
    return _matmul(act, w_down.astype(BF16), F32, epilogue="residual", extra=(x,), tm=512)


def _rwkv_layer(x, v_first, vres_w, norm_g, mu, w0, w1, w2, a0, a1, a2, g1, g2, k_k, k_a, r_k,
                w_r, w_k, w_v, w_o, gn_w, gn_b, batch, seq):
    d = x.shape[1]
    xr, xw, xk, xv, xa, xg = _rms_mix(x, norm_g, mu, seq)
    r = _matmul(xr, w_r.astype(BF16), F32)
    k = _matmul(xk, w_k.astype(BF16), F32)
    v = _matmul(xv, w_v.astype(BF16), F32)
    hw = _matmul(xw, w1.astype(BF16), BF16, epilogue="tanh")
    ha = _matmul(xa, a1.astype(BF16), BF16)
    hg = _matmul(xg, g1.astype(BF16), BF16, epilogue="sigmoid")
    zeros = jnp.zeros((d,), F32)
    if vres_w is None:
        vec = jnp.stack([w0, a0, k_k, k_a, zeros, zeros, zeros, zeros])
        vres = None
    else:
        v0, v1, v2 = vres_w
        vec = jnp.stack([w0, a0, k_k, k_a, v0, zeros, zeros, zeros])
        vres = (_matmul(xv, v1.astype(BF16), BF16), v2.astype(BF16), v_first)
    lw, k2, v2_, kk, a, g = _rwkv_prep(k, v, hw, ha, hg, w2.astype(BF16), a2.astype(BF16), g2.astype(BF16),
                                       vec, vres)
    if vres_w is None:
        v_first = v2_
    out = _rwkv_scan(r, lw, k2, v2_, kk, a, g, gn_w, gn_b, r_k, batch, seq)
    return _matmul(out, w_o.astype(BF16), F32, epilogue="residual", extra=(x,)), v_first


def _attn_layer(x, h, kv, w_q, q_gain, w_o, biases, batch, seq, heads, groups):
    gain = jnp.broadcast_to(q_gain[:, None, :], (N_BR, heads, ATT_HEAD)).reshape(1, -1)
    q = _matmul(h, w_q.astype(BF16), F32, epilogue="headnorm", extra=(gain, jnp.ones_like(gain)), tn=1024)
    o = _attention(q, kv, biases, batch, seq, heads, groups)
    return _matmul(o, w_o.astype(BF16), F32, epilogue="residual", extra=(x,))


def kernel(x, norm_mix, norm_ffn, rwkv_mu, rwkv_w0, rwkv_w1, rwkv_w2, rwkv_a0, rwkv_a1, rwkv_a2, rwkv_v0, rwkv_v1, rwkv_v2, rwkv_g1, rwkv_g2, rwkv_k_k, rwkv_k_a, rwkv_r_k, rwkv_w_r, rwkv_w_k, rwkv_w_v, rwkv_w_o, rwkv_gn_w, rwkv_gn_b, norm_kv, attn_w_kv, attn_k_gain, attn_w_q, attn_q_gain, attn_w_o, rel_bias, ffn_w_up, ffn_conv_w, ffn_conv_b, ffn_w_down):
    batch, seq, d = x.shape
    depth = norm_mix.shape[0]
    n_a = rwkv_mu.shape[0]
    heads = attn_w_o.shape[1] // ATT_HEAD
    groups = attn_w_kv.shape[1] // (N_BR * 2 * ATT_HEAD)
    assert seq % SPAN == 0 and d % LANES == 0 and heads % groups == 0 and (heads // groups) % ATT_HPS == 0
    x = x.reshape(batch * seq, d)
    v_first = None
    kv = None
    biases = None
    for layer in range(depth):
        if layer < n_a:
            vres_w = None if layer == 0 else (rwkv_v0[layer - 1], rwkv_v1[layer - 1], rwkv_v2[layer - 1])
            x, v_first = _rwkv_layer(
                x, v_first, vres_w, norm_mix[layer], rwkv_mu[layer], rwkv_w0[layer], rwkv_w1[layer],
                rwkv_w2[layer], rwkv_a0[layer], rwkv_a1[layer], rwkv_a2[layer], rwkv_g1[layer],
                rwkv_g2[layer], rwkv_k_k[layer], rwkv_k_a[layer], rwkv_r_k[layer], rwkv_w_r[layer],
                rwkv_w_k[layer], rwkv_w_v[layer], rwkv_w_o[layer], rwkv_gn_w[layer], rwkv_gn_b[layer],
                batch, seq)
        else:
            if layer == n_a:
                k_gain = jnp.broadcast_to(attn_k_gain[:, None, None, :], (N_BR, 2, groups, ATT_HEAD))
                k_flag = jnp.broadcast_to(jnp.array([1.0, 0.0], F32)[None, :, None, None],
                                          (N_BR, 2, groups, ATT_HEAD))
                kv = _matmul(_rmsnorm(x, norm_kv), attn_w_kv.astype(BF16), F32, epilogue="headnorm",
                             extra=(k_gain.reshape(1, -1), k_flag.reshape(1, -1)), tn=1024)
                biases = [_bias_table(rel_bias, w, dl) for w, dl in BRANCHES]
            j = layer - n_a
            h = _rmsnorm(x, norm_mix[layer])
            x = _attn_layer(x, h, kv, attn_w_q[j], attn_q_gain[j], attn_w_o[j], biases, batch, seq, heads, groups)
        x = _conv_ffn_layer(x, norm_ffn[layer], ffn_w_up[layer], ffn_conv_w[layer], ffn_conv_b[layer],
                            ffn_w_down[layer], seq)
    return x.reshape(batch, seq, d)
```

```python
import functools

import numpy as np
import jax
import jax.numpy as jnp
from jax import lax
from jax.experimental import pallas as pl
from jax.experimental.pallas import tpu as pltpu

F32 = jnp.float32
BF16 = jnp.bfloat16

RWKV_HEAD = 64
ATT_HEAD = 128
N_BR = 3
BRANCHES = ((128, 1), (512, 4), (2048, 16))
BLOCK = 128
REL_BUCKETS = 32
REL_MAX_DIST = 2048
CONV_W = 3
NORM_EPS = 1e-6
GN_EPS = 64e-5
L2_EPS = 1e-12

LANES = 128
SUBLANES = 8
VMEM_LIMIT_BYTES = 56 * 1024 * 1024

SCAN_CHUNK = 64


def _cparams(*sem):
    return pltpu.CompilerParams(dimension_semantics=sem, vmem_limit_bytes=VMEM_LIMIT_BYTES)


def _pick(n, pref):
    t = min(pref, n)
    while n % t:
        t //= 2
    return t


def _rms(x, g):
    return x * lax.rsqrt(jnp.mean(x * x, axis=-1, keepdims=True) + NORM_EPS) * g


def _dot(a, b):
    return jnp.dot(a, b, preferred_element_type=F32)


def _dot_nt(a, b):
    return lax.dot_general(a, b, (((1,), (1,)), ((), ())), preferred_element_type=F32)


def _dot_tn(a, b):
    return lax.dot_general(a, b, (((0,), (0,)), ((), ())), preferred_element_type=F32)


def _shift_rows(u, halo, n):
    rolled = pltpu.roll(u, n, 0)
    row = lax.broadcasted_iota(jnp.int32, u.shape, 0)
    for r in range(n):
        rolled = jnp.where(row == r, halo[SUBLANES - n + r:SUBLANES - n + r + 1, :], rolled)
    return rolled


def _rms_kernel(x_ref, g_ref, o_ref):
    o_ref[...] = _rms(x_ref[...], g_ref[...]).astype(o_ref.dtype)


def _rmsnorm(x, g, tm=256):
    t, d = x.shape
    tm = _pick(t, tm)
    return pl.pallas_call(
        _rms_kernel,
        grid=(t // tm,),
        in_specs=[pl.BlockSpec((tm, d), lambda i: (i, 0)), pl.BlockSpec((1, d), lambda i: (0, 0))],
        out_specs=pl.BlockSpec((tm, d), lambda i: (i, 0)),
        out_shape=jax.ShapeDtypeStruct((t, d), BF16),
        compiler_params=_cparams("parallel"),
        name="rmsnorm",
    )(x, g.reshape(1, d))


def _rms_mix_kernel(x_ref, halo_ref, g_ref, mu_ref, *o_refs, tiles_per_seq):
    i = pl.program_id(0)
    g = g_ref[...]
    h = _rms(x_ref[...], g)
    hh = _rms(halo_ref[...], g)
    hh = jnp.where(i % tiles_per_seq == 0, 0.0, hh)
    xx = _shift_rows(h, hh, 1) - h
    for n, o_ref in enumerate(o_refs):
        o_ref[...] = (h + xx * mu_ref[n:n + 1, :]).astype(o_ref.dtype)


def _rms_mix(x, g, mu, seq, tm=256):
    t, d = x.shape
    tm = _pick(seq, tm)
    nmix = mu.shape[0]
    hb = tm // SUBLANES
    return pl.pallas_call(
        functools.partial(_rms_mix_kernel, tiles_per_seq=seq // tm),
        grid=(t // tm,),
        in_specs=[
            pl.BlockSpec((tm, d), lambda i: (i, 0)),
            pl.BlockSpec((SUBLANES, d), lambda i: (jnp.maximum(i * hb - 1, 0), 0)),
            pl.BlockSpec((1, d), lambda i: (0, 0)),
            pl.BlockSpec((nmix, d), lambda i: (0, 0)),
        ],
        out_specs=[pl.BlockSpec((tm, d), lambda i: (i, 0))] * nmix,
        out_shape=[jax.ShapeDtypeStruct((t, d), BF16)] * nmix,
        compiler_params=_cparams("parallel"),
        name="rms_mix",
    )(x, x, g.reshape(1, d), mu)


def _headnorm(acc, gain, flag):
    outs = []
    for h in range(acc.shape[1] // ATT_HEAD):
        sl = slice(h * ATT_HEAD, (h + 1) * ATT_HEAD)
        y = acc[:, sl]
        outs.append(jnp.where(flag[:, sl] != 0.0, _rms(y, gain[:, sl]), y))
    return jnp.concatenate(outs, axis=1) if len(outs) > 1 else outs[0]


def _mm_kernel(a_ref, w_ref, *rest, epilogue):
    o_ref = rest[-1]
    acc = _dot(a_ref[...], w_ref[...].astype(BF16))
    if epilogue == "tanh":
        acc = jnp.tanh(acc)
    elif epilogue == "sigmoid":
        acc = jax.nn.sigmoid(acc)
    elif epilogue == "residual":
        acc = rest[0][...] + acc
    elif epilogue == "headnorm":
        acc = _headnorm(acc, rest[0][...], rest[1][...])
    o_ref[...] = acc.astype(o_ref.dtype)


def _matmul(a, w, out_dtype, *, layer=None, epilogue="none", extra=(), tm=1024, tn=512):
    m, k = a.shape
    n = w.shape[-1]
    tm = _pick(m, tm)
    tn = n if n % LANES else _pick(n, tn)
    if layer is None:
        w_spec = pl.BlockSpec((k, tn), lambda i, j: (0, j))
    else:
        w_spec = pl.BlockSpec((None, k, tn), lambda i, j: (layer, 0, j))
    in_specs = [pl.BlockSpec((tm, k), lambda i, j: (i, 0)), w_spec]
    for e in extra:
        if e.shape[0] == 1:
            in_specs.append(pl.BlockSpec((1, tn), lambda i, j: (0, j)))
        else:
            in_specs.append(pl.BlockSpec((tm, tn), lambda i, j: (i, j)))
    return pl.pallas_call(
        functools.partial(_mm_kernel, epilogue=epilogue),
        grid=(m // tm, n // tn),
        in_specs=in_specs,
        out_specs=pl.BlockSpec((tm, tn), lambda i, j: (i, j)),
        out_shape=jax.ShapeDtypeStruct((m, n), out_dtype),
        compiler_params=_cparams("parallel", "arbitrary"),
        name="matmul_" + epilogue,
    )(a, w, *extra)


def _ffn_up_kernel(h_ref, halo_ref, wg_ref, wv_ref, cwg_ref, cwv_ref, cbg_ref, cbv_ref, o_ref, *, tiles_per_seq):
    i = pl.program_id(0)
    first = i % tiles_per_seq == 0
    h = h_ref[...]
    halo = halo_ref[...]

    def conv(w_ref, cw_ref, cb_ref):
        w = w_ref[...].astype(BF16)
        u = _dot(h, w)
        uh = jnp.where(first, 0.0, _dot(halo, w))
        cw = cw_ref[...]
        return (cw[2:3, :] * u + cw[1:2, :] * _shift_rows(u, uh, 1)
                + cw[0:1, :] * _shift_rows(u, uh, 2) + cb_ref[...])

    gate = conv(wg_ref, cwg_ref, cbg_ref)
    val = conv(wv_ref, cwv_ref, cbv_ref)
    o_ref[...] = (jax.nn.silu(gate) * val).astype(o_ref.dtype)


def _ffn_up(h, w_up, conv_w, conv_b, layer, seq, tm=1024, tn=256):
    t, k = h.shape
    f = w_up.shape[-1] // 2
    tm = _pick(seq, tm)
    tn = _pick(f, tn)
    nj = f // tn
    hb = tm // SUBLANES
    return pl.pallas_call(
        functools.partial(_ffn_up_kernel, tiles_per_seq=seq // tm),
        grid=(t // tm, nj),
        in_specs=[
            pl.BlockSpec((tm, k), lambda i, j: (i, 0)),
            pl.BlockSpec((SUBLANES, k), lambda i, j: (jnp.maximum(i * hb - 1, 0), 0)),
            pl.BlockSpec((None, k, tn), lambda i, j: (layer, 0, j)),
            pl.BlockSpec((None, k, tn), lambda i, j: (layer, 0, j + nj)),
            pl.BlockSpec((None, CONV_W, tn), lambda i, j: (layer, 0, j)),
            pl.BlockSpec((None, CONV_W, tn), lambda i, j: (layer, 0, j + nj)),
            pl.BlockSpec((None, 1, tn), lambda i, j: (layer, 0, j)),
            pl.BlockSpec((None, 1, tn), lambda i, j: (layer, 0, j + nj)),
        ],
        out_specs=pl.BlockSpec((tm, tn), lambda i, j: (i, j)),
        out_shape=jax.ShapeDtypeStruct((t, f), BF16),
        compiler_params=_cparams("parallel", "arbitrary"),
        name="ffn_up",
    )(h, h, w_up, w_up, conv_w, conv_w, conv_b, conv_b)


def _halfsum(x):
    outs = []
    for gidx in range(x.shape[1] // LANES):
        y = x[:, gidx * LANES:(gidx + 1) * LANES]
        lo = lax.broadcasted_iota(jnp.int32, y.shape, 1) < RWKV_HEAD
        s_lo = jnp.sum(jnp.where(lo, y, 0.0), axis=-1, keepdims=True)
        s_hi = jnp.sum(jnp.where(lo, 0.0, y), axis=-1, keepdims=True)
        outs.append(jnp.where(lo, s_lo, s_hi))
    return jnp.concatenate(outs, axis=1) if len(outs) > 1 else outs[0]


def _rwkv_prep_kernel(*refs, has_vres):
    if has_vres:
        (k_ref, v_ref, hw_ref, ha_ref, hg_ref, w2_ref, a2_ref, g2_ref, vec_ref,
         hv_ref, v2_ref, vf_ref, lw_ref, ko_ref, vo_ref, kk_ref, a_ref, g_ref) = refs
    else:
        (k_ref, v_ref, hw_ref, ha_ref, hg_ref, w2_ref, a2_ref, g2_ref, vec_ref,
         lw_ref, ko_ref, vo_ref, kk_ref, a_ref, g_ref) = refs
    vec = vec_ref[...]
    w0, a0, k_k, k_a, v0 = (vec[n:n + 1, :] for n in range(5))
    k = k_ref[...].astype(F32)
    v = v_ref[...].astype(F32)
    lw_ref[...] = -np.exp(-0.5).astype(np.float32) * jax.nn.sigmoid(w0 + _dot(hw_ref[...], w2_ref[...]))
    a = jax.nn.sigmoid(a0 + _dot(ha_ref[...], a2_ref[...]))
    g_ref[...] = _dot(hg_ref[...], g2_ref[...]).astype(g_ref.dtype)
    if has_vres:
        v = v + (vf_ref[...].astype(F32) - v) * jax.nn.sigmoid(v0 + _dot(hv_ref[...], v2_ref[...]))
    kk = k * k_k
    kk = kk / jnp.maximum(jnp.sqrt(_halfsum(kk * kk)), L2_EPS)
    ko_ref[...] = (k * (1.0 + (a - 1.0) * k_a)).astype(ko_ref.dtype)
    vo_ref[...] = v.astype(vo_ref.dtype)
    kk_ref[...] = kk.astype(kk_ref.dtype)
    a_ref[...] = a.astype(a_ref.dtype)


def _rwkv_prep(k, v, hw, ha, hg, w2, a2, g2, vec, vres, tm=256, tn=512):
    t, d = k.shape
    tm = _pick(t, tm)
    tn = _pick(d, tn)
    tile = pl.BlockSpec((tm, tn), lambda i, j: (i, j))
    rows = lambda arr: pl.BlockSpec((tm, arr.shape[1]), lambda i, j: (i, 0))
    cols = lambda arr: pl.BlockSpec((arr.shape[0], tn), lambda i, j: (0, j))
    args = [k, v, hw, ha, hg, w2, a2, g2, vec]
    in_specs = [tile, tile, rows(hw), rows(ha), rows(hg), cols(w2), cols(a2), cols(g2), cols(vec)]
    if vres is not None:
        hv, v2, v_first = vres
        args += [hv, v2, v_first]
        in_specs += [rows(hv), cols(v2), tile]
    return pl.pallas_call(
        functools.partial(_rwkv_prep_kernel, has_vres=vres is not None),
        grid=(t // tm, d // tn),
        in_specs=in_specs,
        out_specs=[tile] * 6,
        out_shape=[jax.ShapeDtypeStruct((t, d), F32)] + [jax.ShapeDtypeStruct((t, d), BF16)] * 5,
        compiler_params=_cparams("parallel", "parallel"),
        name="rwkv_prep",
    )(*args)


def _pair_masks():
    c = SCAN_CHUNK
    lane = lax.broadcasted_iota(jnp.int32, (c, LANES), 1)
    head0 = lane < RWKV_HEAD
    i = lax.broadcasted_iota(jnp.int32, (2 * c, 2 * c), 0)
    j = lax.broadcasted_iota(jnp.int32, (2 * c, 2 * c), 1)
    same = (i < c) == (j < c)
    ti = jnp.where(i < c, i, i - c)
    tj = jnp.where(j < c, j, j - c)
    row = lax.broadcasted_iota(jnp.int32, (c, c), 0)
    col = lax.broadcasted_iota(jnp.int32, (c, c), 1)
    return dict(head0=head0, strict=same & (ti > tj), incl=same & (ti >= tj), same=same,
                eye=(i == j).astype(F32), tri=(row >= col).astype(BF16))


def _split_heads(x, head0):
    return jnp.concatenate([jnp.where(head0, x, 0.0), jnp.where(head0, 0.0, x)], axis=0)


def _merge_heads(x, head0):
    c = x.shape[0] // 2
    return jnp.where(head0, x[:c], x[c:])


def _twice(x):
    return jnp.concatenate([x, x], axis=0)


def _cumsum_rows(x, tri):
    x0 = x.astype(BF16)
    e1 = x - x0.astype(F32)
    x1 = e1.astype(BF16)
    x2 = (e1 - x1.astype(F32)).astype(BF16)
    return _dot(tri, x0) + _dot(tri, x1) + _dot(tri, x2)


def _scan_inv_kernel(lw_ref, kk_ref, a_ref, t_ref, cum_ref, *, chunks, pairs):
    c = SCAN_CHUNK
    mk = _pair_masks()
    head0 = mk["head0"]
    units = [(slice(j * c, (j + 1) * c), slice(pi * LANES, (pi + 1) * LANES))
             for pi in range(pairs) for j in range(chunks)]
    nu = len(units)
    lw = [lw_ref[u] for u in units]
    kk = [kk_ref[u].astype(F32) for u in units]
    cum = [_cumsum_rows(x, mk["tri"]) for x in lw]
    for n, u in enumerate(units):
        cum_ref[u] = cum[n]
    at = [_split_heads(-kk[n] * jnp.exp(cum[n] - lw[n]), head0).astype(BF16) for n in range(nu)]
    bt = [_twice(kk[n] * a_ref[u].astype(F32) * jnp.exp(-cum[n])).astype(BF16) for n, u in enumerate(units)]
    ab = [jnp.where(mk["strict"], _dot_nt(at[n], bt[n]), 0.0) for n in range(nu)]
    t = [mk["eye"] + x for x in ab]
    pw = [x.astype(BF16) for x in ab]
    pw = [_dot(x, x).astype(BF16) for x in pw]
    doublings = int(np.log2(c)) - 1
    for it in range(doublings):
        t = [t[n] + _dot(t[n].astype(BF16), pw[n]) for n in range(nu)]
        if it < doublings - 1:
            pw = [_dot(x, x).astype(BF16) for x in pw]
    for n, u in enumerate(units):
        t_ref[u] = (t[n][:c] + t[n][c:]).astype(t_ref.dtype)


def _scan_kernel(r_ref, cum_ref, lw_ref, k_ref, v_ref, kk_ref, a_ref, g_ref, t_ref, gnw_ref, gnb_ref, rk_ref,
                 o_ref, s_ref, *, chunks, pairs):
    c = SCAN_CHUNK
    hd = RWKV_HEAD

    @pl.when(pl.program_id(2) == 0)
    def _():
        s_ref[...] = jnp.zeros_like(s_ref)

    mk = _pair_masks()
    head0 = mk["head0"]
    lanes = [slice(pi * LANES, (pi + 1) * LANES) for pi in range(pairs)]

    def per_head(x):
        s0 = jnp.sum(jnp.where(head0, x, 0.0), axis=-1, keepdims=True)
        s1 = jnp.sum(jnp.where(head0, 0.0, x), axis=-1, keepdims=True)
        return jnp.where(head0, s0, s1)

    def prepare(j):
        rows = slice(j * c, (j + 1) * c)
        pre = []
        for ls in lanes:
            u = (rows, ls)
            cum = cum_ref[u]
            k = k_ref[u].astype(F32)
            kk = kk_ref[u].astype(F32)
            b = kk * a_ref[u].astype(F32)
            at = -kk * jnp.exp(cum - lw_ref[u])
            rt = r_ref[u].astype(F32) * jnp.exp(cum)
            pinv = jnp.exp(-cum)
            to_end = jnp.exp(cum[c - 1:c, :] - cum)
            lhs = jnp.concatenate([_split_heads(at, head0), _split_heads(rt, head0)], axis=0).astype(BF16)
            pre.append(dict(
                u=u, lhs=lhs, v16=v_ref[u], p_end=jnp.exp(cum[c - 1:c, :]),
                kt=_twice(k * pinv).astype(BF16), bt=_twice(b * pinv).astype(BF16),
                ar=jnp.concatenate([at, rt], axis=0).astype(BF16),
                kb_end=jnp.concatenate([(k * to_end).astype(BF16), (b * to_end).astype(BF16)], axis=0),
                tinv=_split_heads(t_ref[u].astype(F32), head0).astype(BF16)))
        return pre

    def scores_k(pre):
        for q in pre:
            q["sk"] = _dot_nt(q["lhs"], q["kt"])

    def scores_b(pre):
        for q in pre:
            sb = _dot_nt(q["lhs"][2 * c:], q["bt"])
            ak = jnp.where(mk["strict"], q["sk"][:2 * c], 0.0).astype(BF16)
            q["akv"] = _merge_heads(_dot(ak, _twice(q["v16"])), head0)
            q["rkb"] = jnp.concatenate([jnp.where(mk["incl"], q["sk"][2 * c:], 0.0),
                                        jnp.where(mk["incl"], sb, 0.0)], axis=1).astype(BF16)

    def finish(pre):
        for q in pre:
            u = q["u"]
            ls = u[1]
            vu2 = jnp.concatenate([_twice(q["v16"]), _twice(q["u16"])], axis=0)
            y = q["zs_r"] + _merge_heads(_dot(q["rkb"], vu2), head0)
            mean = per_head(y) * (1.0 / hd)
            yc = y - mean
            var = per_head(yc * yc) * (1.0 / hd)
            yn = yc * lax.rsqrt(var + GN_EPS) * gnw_ref[:, ls] + gnb_ref[:, ls]
            bonus = per_head(r_ref[u].astype(F32) * k_ref[u].astype(F32) * rk_ref[:, ls]) * v_ref[u].astype(F32)
            o_ref[u] = ((yn + bonus) * g_ref[u].astype(F32)).astype(o_ref.dtype)

    s = [s_ref[pi] for pi in range(pairs)]
    cur = prepare(0)
    scores_k(cur)
    scores_b(cur)
    prev = None
    for j in range(chunks):
        nxt = prepare(j + 1) if j + 1 < chunks else None
        zs = [_dot_nt(q["ar"], s[pi].astype(BF16)) for pi, q in enumerate(cur)]
        if nxt is not None:
            scores_k(nxt)
        for pi, q in enumerate(cur):
            z16 = (zs[pi][:c] + q["akv"]).astype(BF16)
            q["u16"] = _merge_heads(_dot(q["tinv"], _twice(z16)), head0).astype(BF16)
            q["zs_r"] = zs[pi][c:]
        if nxt is not None:
            scores_b(nxt)
        for pi, q in enumerate(cur):
            vu = jnp.concatenate([q["v16"], q["u16"]], axis=0)
            s[pi] = s[pi] * q["p_end"] + jnp.where(mk["same"], _dot_tn(vu, q["kb_end"]), 0.0)
        if prev is not None:
            finish(prev)
        prev, cur = cur, nxt
    finish(prev)
    for pi in range(pairs):
        s_ref[pi] = s[pi]


def _rwkv_scan(r, lw, k, v, kk, a, g, gn_w, gn_b, r_k, batch, seq, tc=256, pairs=4, tc_inv=256):
    t, d = r.shape
    tc = _pick(seq, tc)
    tc_inv = _pick(t, tc_inv)
    pairs = min(pairs, d // LANES)
    w = pairs * LANES
    itile = pl.BlockSpec((tc_inv, w), lambda i, pi: (i, pi))
    tinv, cum = pl.pallas_call(
        functools.partial(_scan_inv_kernel, chunks=tc_inv // SCAN_CHUNK, pairs=pairs),
        grid=(t // tc_inv, d // w),
        in_specs=[itile] * 3,
        out_specs=[itile] * 2,
        out_shape=[jax.ShapeDtypeStruct((t, d), BF16), jax.ShapeDtypeStruct((t, d), F32)],
        compiler_params=_cparams("parallel", "parallel"),
        name="rwkv_inv",
    )(lw, kk, a)
    steps = seq // tc
    tile = pl.BlockSpec((tc, w), lambda bi, pi, ci: (bi * steps + ci, pi))
    vec = pl.BlockSpec((1, w), lambda bi, pi, ci: (0, pi))
    return pl.pallas_call(
        functools.partial(_scan_kernel, chunks=tc // SCAN_CHUNK, pairs=pairs),
        grid=(batch, d // w, steps),
        in_specs=[tile] * 9 + [vec] * 3,
        out_specs=tile,
        out_shape=jax.ShapeDtypeStruct((t, d), BF16),
        scratch_shapes=[pltpu.VMEM((pairs, LANES, LANES), F32)],
        compiler_params=_cparams("parallel", "parallel", "arbitrary"),
        name="rwkv_scan",
    )(r, cum, lw, k, v, kk, a, g, tinv, gn_w.reshape(1, d), gn_b.reshape(1, d), r_k.reshape(1, d))


SPAN = BLOCK * BRANCHES[-1][1]
ATT_HPS = 2
UNITS_PER_ITER = 4


def _attn_kernel(*refs):
    nbr = len(BRANCHES)
    q_refs = [refs[i * ATT_HPS:(i + 1) * ATT_HPS] for i in range(nbr)]
    base = nbr * ATT_HPS
    kc_refs, vc_refs, kp_refs, vp_refs, bias_refs = (refs[base + j * nbr: base + (j + 1) * nbr] for j in range(5))
    o_ref, acc_ref, m_ref, l_ref, stage_ref = refs[base + 5 * nbr:]
    blk = BLOCK
    n = pl.program_id(1)
    scale = ATT_HEAD ** -0.5
    kj = lax.broadcasted_iota(jnp.int32, (ATT_HPS * blk, 2 * blk), 1)
    widest = nbr - 1
    d_max = BRANCHES[widest][1]
    sub = 4
    assert d_max == sub * sub and BRANCHES[1][1] == sub and BRANCHES[0][1] == 1
    run = SPAN // sub

    staged = [q_refs[widest][e] for e in range(ATT_HPS)] + [kc_refs[widest], vc_refs[widest],
                                                           kp_refs[widest], vp_refs[widest]]
    for a, src in enumerate(staged):
        for r4 in range(sub):
            stage_ref[a, r4 * run:(r4 + 1) * run, :] = src[pl.ds(r4, run, stride=sub), :]

    def load_unit(i, d, u):
        if i == widest:
            r_lo = u % sub
            off = r_lo * run + u // sub
            rows_s = pl.ds(off, blk, stride=sub)
            q = jnp.concatenate([stage_ref[e, rows_s, :] for e in range(ATT_HPS)], axis=0)
            kcat = jnp.concatenate([stage_ref[ATT_HPS + 2, rows_s, :], stage_ref[ATT_HPS, rows_s, :]], axis=0)
            vcat = jnp.concatenate([stage_ref[ATT_HPS + 3, rows_s, :], stage_ref[ATT_HPS + 1, rows_s, :]], axis=0)
            return q, kcat, vcat, pl.ds(u, blk, stride=d), True
        nb = u // d
        start = nb * (blk * d) + (u - nb * d)
        rows = pl.ds(start, blk, stride=d)
        q = jnp.concatenate([q_refs[i][e][rows, :] for e in range(ATT_HPS)], axis=0)
        prow_in_cur = pl.ds(jnp.maximum(start - blk * d, 0), blk, stride=d)
        prow_in_prev = pl.ds(u - nb * d, blk, stride=d)
        first = nb == 0
        kprev = jnp.where(first, kp_refs[i][prow_in_prev, :], kc_refs[i][prow_in_cur, :])
        vprev = jnp.where(first, vp_refs[i][prow_in_prev, :], vc_refs[i][prow_in_cur, :])
        kcat = jnp.concatenate([kprev, kc_refs[i][rows, :]], axis=0)
        vcat = jnp.concatenate([vprev, vc_refs[i][rows, :]], axis=0)
        return q, kcat, vcat, rows, first

    for order, i in enumerate(reversed(range(nbr))):
        d = BRANCHES[i][1]

        def body(it, carry, i=i, d=d, init=order == 0):
            loaded = [load_unit(i, d, it * UNITS_PER_ITER + x) for x in range(UNITS_PER_ITER)]
            bias = bias_refs[i][...].reshape(ATT_HPS * blk, 2 * blk)
            ss = []
            for q, kcat, _, _, first in loaded:
                s = _dot_nt((q * scale).astype(BF16), kcat.astype(BF16)) + bias
                ss.append(jnp.where((n == 0) & first & (kj < blk), -jnp.inf, s))
            ms = [jnp.max(s, axis=-1, keepdims=True) for s in ss]
            ps = [jnp.exp(s - m) for s, m in zip(ss, ms)]
            ls = [jnp.sum(p, axis=-1, keepdims=True) for p in ps]
            os_ = [_dot(p.astype(BF16), ld[2].astype(BF16)) for p, ld in zip(ps, loaded)]
            for x in range(UNITS_PER_ITER):
                rows = loaded[x][3]
                for e in range(ATT_HPS):
                    hr = slice(e * blk, (e + 1) * blk)
                    m_new = jnp.broadcast_to(ms[x][hr], (blk, ATT_HEAD))
                    l_new = jnp.broadcast_to(ls[x][hr], (blk, ATT_HEAD))
                    o_new = os_[x][hr]
                    if init:
                        m_ref[e, rows, :] = m_new
                        l_ref[e, rows, :] = l_new
                        acc_ref[e, rows, :] = o_new
                    else:
                        m_old = m_ref[e, rows, :]
                        m2 = jnp.maximum(m_old, m_new)
                        w_old = jnp.exp(m_old - m2)
                        w_new = jnp.exp(m_new - m2)
                        m_ref[e, rows, :] = m2
                        l_ref[e, rows, :] = l_ref[e, rows, :] * w_old + l_new * w_new
                        acc_ref[e, rows, :] = acc_ref[e, rows, :] * w_old + o_new * w_new
            return carry

        lax.fori_loop(0, (SPAN // blk) // UNITS_PER_ITER, body, 0)

    for e in range(ATT_HPS):
        o_ref[:, e * ATT_HEAD:(e + 1) * ATT_HEAD] = (acc_ref[e] / l_ref[e]).astype(o_ref.dtype)


def _attention(q, kv, biases, batch, seq, heads, groups):
    t = q.shape[0]
    nsb = seq // SPAN
    per_kv = heads // groups
    steps = heads // ATT_HPS
    in_specs, args = [], []
    for i in range(N_BR):
        for e in range(ATT_HPS):
            in_specs.append(pl.BlockSpec((SPAN, ATT_HEAD),
                                         lambda b, n, hp, i=i, e=e: (b * nsb + n, i * heads + hp * ATT_HPS + e)))
            args.append(q)
    kvcol = lambda i, sel, hp: (i * 2 + sel) * groups + (hp * ATT_HPS) // per_kv
    for sel in range(2):
        for i in range(N_BR):
            in_specs.append(pl.BlockSpec((SPAN, ATT_HEAD),
                                         lambda b, n, hp, i=i, sel=sel: (b * nsb + n, kvcol(i, sel, hp))))
            args.append(kv)
    for sel in range(2):
        for i, (_, d) in enumerate(BRANCHES):
            rows = BLOCK * d
            in_specs.append(pl.BlockSpec(
                (rows, ATT_HEAD),
                lambda b, n, hp, i=i, sel=sel, rows=rows: (jnp.maximum((b * nsb + n) * (SPAN // rows) - 1, 0),
                                                            kvcol(i, sel, hp))))
            args.append(kv)
    for i in range(N_BR):
        in_specs.append(pl.BlockSpec((ATT_HPS, BLOCK, 2 * BLOCK), lambda b, n, hp: (hp, 0, 0)))
        args.append(biases[i])
    return pl.pallas_call(
        _attn_kernel,
        grid=(batch, nsb, steps),
        in_specs=in_specs,
        out_specs=pl.BlockSpec((SPAN, ATT_HPS * ATT_HEAD), lambda b, n, hp: (b * nsb + n, hp)),
        out_shape=jax.ShapeDtypeStruct((t, heads * ATT_HEAD), BF16),
        scratch_shapes=[pltpu.VMEM((ATT_HPS, SPAN, ATT_HEAD), F32)] * 3
        + [pltpu.VMEM((ATT_HPS + 4, SPAN, ATT_HEAD), F32)],
        compiler_params=_cparams("parallel", "parallel", "arbitrary"),
        name="dilated_attn",
    )(*args)


def _t5_bucket(distance):
    max_exact = REL_BUCKETS // 2
    dist = np.asarray(distance, dtype=np.int64)
    scaled = np.log(np.maximum(dist, max_exact) / max_exact) / np.log(REL_MAX_DIST / max_exact)
    large = np.minimum(max_exact + (scaled * (REL_BUCKETS - max_exact)).astype(np.int64), REL_BUCKETS - 1)
    return np.where(dist < max_exact, dist, large).astype(np.int32)


def _bias_table(rel_bias, window, dilation):
    band = window // dilation
    qi = np.arange(BLOCK)[:, None]
    kj = np.arange(2 * BLOCK)[None, :]
    rel = qi + BLOCK - kj
    bucket = _t5_bucket(np.clip(rel, 0, band) * dilation)
    bias = jnp.transpose(rel_bias[bucket], (2, 0, 1)).astype(F32)
    return jnp.where(((rel >= 0) & (rel <= band))[None], bias, -jnp.inf)


def _conv_ffn_layer(x, norm_g, w_up, conv_w, conv_b, w_down, layer, seq):
    h = _rmsnorm(x, norm_g)
    act = _ffn_up(h, w_up, conv_w, conv_b[:, None, :], layer, seq)
    return _matmul(act, w_down, F32, layer=layer, epilogue="residual", extra=(x,), tm=512, tn=256)


def _rwkv_layer(x, v_first, layer, norm_g, mu, w0, w1, w2, a0, a1, a2, vres_w, g1, g2, k_k, k_a, r_k,
                w_r, w_k, w_v, w_o, gn_w, gn_b, batch, seq):
    d = x.shape[1]
    xr, xw, xk, xv, xa, xg = _rms_mix(x, norm_g, mu, seq)
    r = _matmul(xr, w_r, BF16, layer=layer)
    k = _matmul(xk, w_k, BF16, layer=layer)
    v = _matmul(xv, w_v, BF16, layer=layer)
    hw = _matmul(xw, w1, BF16, layer=layer, epilogue="tanh")
    ha = _matmul(xa, a1, BF16, layer=layer)
    hg = _matmul(xg, g1, BF16, layer=layer, epilogue="sigmoid")
    zeros = jnp.zeros((d,), F32)
    if vres_w is None:
        vec = jnp.stack([w0, a0, k_k, k_a, zeros, zeros, zeros, zeros])
        vres = None
    else:
        v0, v1, v2 = vres_w
        vec = jnp.stack([w0, a0, k_k, k_a, v0, zeros, zeros, zeros])
        vres = (_matmul(xv, v1, BF16), v2.astype(BF16), v_first)
    lw, k2, v2_, kk, a, g = _rwkv_prep(k, v, hw, ha, hg, w2.astype(BF16), a2.astype(BF16), g2.astype(BF16),
                                       vec, vres)
    if vres_w is None:
        v_first = v2_
    out = _rwkv_scan(r, lw, k2, v2_, kk, a, g, gn_w, gn_b, r_k, batch, seq)
    return _matmul(out, w_o, F32, layer=layer, epilogue="residual", extra=(x,)), v_first


def _attn_layer(x, h, kv, w_q, q_gain, w_o, j, biases, batch, seq, heads, groups):
    gain = jnp.broadcast_to(q_gain[:, None, :], (N_BR, heads, ATT_HEAD)).reshape(1, -1)
    q = _matmul(h, w_q, F32, layer=j, epilogue="headnorm", extra=(gain, jnp.ones_like(gain)))
    o = _attention(q, kv, biases, batch, seq, heads, groups)
    return _matmul(o, w_o, F32, layer=j, epilogue="residual", extra=(x,))


def kernel(x, norm_mix, norm_ffn, rwkv_mu, rwkv_w0, rwkv_w1, rwkv_w2, rwkv_a0, rwkv_a1, rwkv_a2, rwkv_v0, rwkv_v1, rwkv_v2, rwkv_g1, rwkv_g2, rwkv_k_k, rwkv_k_a, rwkv_r_k, rwkv_w_r, rwkv_w_k, rwkv_w_v, rwkv_w_o, rwkv_gn_w, rwkv_gn_b, norm_kv, attn_w_kv, attn_k_gain, attn_w_q, attn_q_gain, attn_w_o, rel_bias, ffn_w_up, ffn_conv_w, ffn_conv_b, ffn_w_down):
    batch, seq, d = x.shape
    depth = norm_mix.shape[0]
    n_a = rwkv_mu.shape[0]
    heads = attn_w_o.shape[-2] // ATT_HEAD
    groups = attn_w_kv.shape[-1] // (N_BR * 2 * ATT_HEAD)
    assert seq % SPAN == 0 and d % LANES == 0 and heads % groups == 0 and (heads // groups) % ATT_HPS == 0
    x = x.reshape(batch * seq, d)
    v_first = None
    kv = None
    biases = None
    w_down16 = ffn_w_down.astype(BF16)
    for layer in range(depth):
        if layer < n_a:
            vres_w = None if layer == 0 else (rwkv_v0[layer - 1], rwkv_v1[layer - 1], rwkv_v2[layer - 1])
            x, v_first = _rwkv_layer(
                x, v_first, layer, norm_mix[layer], rwkv_mu[layer], rwkv_w0[layer], rwkv_w1, rwkv_w2[layer],
                rwkv_a0[layer], rwkv_a1, rwkv_a2[layer], vres_w, rwkv_g1, rwkv_g2[layer], rwkv_k_k[layer],
                rwkv_k_a[layer], rwkv_r_k[layer], rwkv_w_r, rwkv_w_k, rwkv_w_v, rwkv_w_o, rwkv_gn_w[layer],
                rwkv_gn_b[layer], batch, seq)
        else:
            if layer == n_a:
                k_gain = jnp.broadcast_to(attn_k_gain[:, None, None, :], (N_BR, 2, groups, ATT_HEAD))
                k_flag = jnp.broadcast_to(jnp.array([1.0, 0.0], F32)[None, :, None, None],
                                          (N_BR, 2, groups, ATT_HEAD))
                kv = _matmul(_rmsnorm(x, norm_kv), attn_w_kv, F32, epilogue="headnorm",
                             extra=(k_gain.reshape(1, -1), k_flag.reshape(1, -1)))
                biases = [_bias_table(rel_bias, w, dl) for w, dl in BRANCHES]
            j = layer - n_a
            h = _rmsnorm(x, norm_mix[layer])
            x = _attn_layer(x, h, kv, attn_w_q, attn_q_gain[j], attn_w_o, j, biases, batch, seq, heads, groups)
        x = _conv_ffn_layer(x, norm_ffn[layer], ffn_w_up, ffn_conv_w, ffn_conv_b, w_down16, layer, seq)
    return x.reshape(batch, seq, d)
```

```python
import functools

import numpy as np
import jax
import jax.numpy as jnp
from jax import lax
from jax.experimental import pallas as pl
from jax.experimental.pallas import tpu as pltpu

F32 = jnp.float32
BF16 = jnp.bfloat16

RWKV_HEAD = 64
ATT_HEAD = 128
N_BR = 3
BRANCHES = ((128, 1), (512, 4), (2048, 16))
BLOCK = 128
REL_BUCKETS = 32
REL_MAX_DIST = 2048
CONV_W = 3
NORM_EPS = 1e-6
GN_EPS = 64e-5
L2_EPS = 1e-12

LANES = 128
SUBLANES = 8
VMEM_LIMIT_BYTES = 56 * 1024 * 1024

SCAN_CHUNK = 64


def _cparams(*sem):
    return pltpu.CompilerParams(dimension_semantics=sem, vmem_limit_bytes=VMEM_LIMIT_BYTES)


def _pick(n, pref):
    t = min(pref, n)
    while n % t:
        t //= 2
    return t


def _rms(x, g):
    return x * lax.rsqrt(jnp.mean(x * x, axis=-1, keepdims=True) + NORM_EPS) * g


def _dot(a, b):
    return jnp.dot(a, b, preferred_element_type=F32)


def _dot_nt(a, b):
    return lax.dot_general(a, b, (((1,), (1,)), ((), ())), preferred_element_type=F32)


def _dot_tn(a, b):
    return lax.dot_general(a, b, (((0,), (0,)), ((), ())), preferred_element_type=F32)


def _shift_rows(u, halo, n):
    rolled = pltpu.roll(u, n, 0)
    row = lax.broadcasted_iota(jnp.int32, u.shape, 0)
    for r in range(n):
        rolled = jnp.where(row == r, halo[SUBLANES - n + r:SUBLANES - n + r + 1, :], rolled)
    return rolled


def _rms_kernel(x_ref, g_ref, o_ref):
    o_ref[...] = _rms(x_ref[...], g_ref[...]).astype(o_ref.dtype)


def _rmsnorm(x, g, tm=256):
    t, d = x.shape
    tm = _pick(t, tm)
    return pl.pallas_call(
        _rms_kernel,
        grid=(t // tm,),
        in_specs=[pl.BlockSpec((tm, d), lambda i: (i, 0)), pl.BlockSpec((1, d), lambda i: (0, 0))],
        out_specs=pl.BlockSpec((tm, d), lambda i: (i, 0)),
        out_shape=jax.ShapeDtypeStruct((t, d), BF16),
        compiler_params=_cparams("parallel"),
        name="rmsnorm",
    )(x, g.reshape(1, d))


def _rms_mix_kernel(x_ref, halo_ref, g_ref, mu_ref, *o_refs, tiles_per_seq):
    i = pl.program_id(0)
    g = g_ref[...]
    h = _rms(x_ref[...], g)
    hh = _rms(halo_ref[...], g)
    hh = jnp.where(i % tiles_per_seq == 0, 0.0, hh)
    xx = _shift_rows(h, hh, 1) - h
    for n, o_ref in enumerate(o_refs):
        o_ref[...] = (h + xx * mu_ref[n:n + 1, :]).astype(o_ref.dtype)


def _rms_mix(x, g, mu, seq, tm=256):
    t, d = x.shape
    tm = _pick(seq, tm)
    nmix = mu.shape[0]
    hb = tm // SUBLANES
    return pl.pallas_call(
        functools.partial(_rms_mix_kernel, tiles_per_seq=seq // tm),
        grid=(t // tm,),
        in_specs=[
            pl.BlockSpec((tm, d), lambda i: (i, 0)),
            pl.BlockSpec((SUBLANES, d), lambda i: (jnp.maximum(i * hb - 1, 0), 0)),
            pl.BlockSpec((1, d), lambda i: (0, 0)),
            pl.BlockSpec((nmix, d), lambda i: (0, 0)),
        ],
        out_specs=[pl.BlockSpec((tm, d), lambda i: (i, 0))] * nmix,
        out_shape=[jax.ShapeDtypeStruct((t, d), BF16)] * nmix,
        compiler_params=_cparams("parallel"),
        name="rms_mix",
    )(x, x, g.reshape(1, d), mu)


def _headnorm(acc, gain, flag):
    outs = []
    for h in range(acc.shape[1] // ATT_HEAD):
        sl = slice(h * ATT_HEAD, (h + 1) * ATT_HEAD)
        y = acc[:, sl]
        outs.append(jnp.where(flag[:, sl] != 0.0, _rms(y, gain[:, sl]), y))
    return jnp.concatenate(outs, axis=1) if len(outs) > 1 else outs[0]


def _mm_kernel(a_ref, w_ref, *rest, epilogue):
    o_ref = rest[-1]
    acc = _dot(a_ref[...], w_ref[...].astype(BF16))
    if epilogue == "tanh":
        acc = jnp.tanh(acc)
    elif epilogue == "sigmoid":
        acc = jax.nn.sigmoid(acc)
    elif epilogue == "residual":
        acc = rest[0][...] + acc
    elif epilogue == "headnorm":
        acc = _headnorm(acc, rest[0][...], rest[1][...])
    o_ref[...] = acc.astype(o_ref.dtype)


def _matmul(a, w, out_dtype, *, layer=None, epilogue="none", extra=(), tm=1024, tn=512):
    m, k = a.shape
    n = w.shape[-1]
    tm = _pick(m, tm)
    tn = n if n % LANES else _pick(n, tn)
    if layer is None:
        w_spec = pl.BlockSpec((k, tn), lambda i, j: (0, j))
    else:
        w_spec = pl.BlockSpec((None, k, tn), lambda i, j: (layer, 0, j))
    in_specs = [pl.BlockSpec((tm, k), lambda i, j: (i, 0)), w_spec]
    for e in extra:
        if e.shape[0] == 1:
            in_specs.append(pl.BlockSpec((1, tn), lambda i, j: (0, j)))
        else:
            in_specs.append(pl.BlockSpec((tm, tn), lambda i, j: (i, j)))
    return pl.pallas_call(
        functools.partial(_mm_kernel, epilogue=epilogue),
        grid=(m // tm, n // tn),
        in_specs=in_specs,
        out_specs=pl.BlockSpec((tm, tn), lambda i, j: (i, j)),
        out_shape=jax.ShapeDtypeStruct((m, n), out_dtype),
        compiler_params=_cparams("parallel", "arbitrary"),
        name="matmul_" + epilogue,
    )(a, w, *extra)


def _ffn_up_kernel(h_ref, halo_ref, wg_ref, wv_ref, cwg_ref, cwv_ref, cbg_ref, cbv_ref, o_ref, *, tiles_per_seq):
    i = pl.program_id(0)
    first = i % tiles_per_seq == 0
    h = h_ref[...]
    halo = halo_ref[...]

    def conv(w_ref, cw_ref, cb_ref):
        w = w_ref[...].astype(BF16)
        u = _dot(h, w)
        uh = jnp.where(first, 0.0, _dot(halo, w))
        cw = cw_ref[...]
        return (cw[2:3, :] * u + cw[1:2, :] * _shift_rows(u, uh, 1)
                + cw[0:1, :] * _shift_rows(u, uh, 2) + cb_ref[...])

    gate = conv(wg_ref, cwg_ref, cbg_ref)
    val = conv(wv_ref, cwv_ref, cbv_ref)
    o_ref[...] = (jax.nn.silu(gate) * val).astype(o_ref.dtype)


def _ffn_up(h, w_up, conv_w, conv_b, layer, seq, tm=1024, tn=256):
    t, k = h.shape
    f = w_up.shape[-1] // 2
    tm = _pick(seq, tm)
    tn = _pick(f, tn)
    nj = f // tn
    hb = tm // SUBLANES
    return pl.pallas_call(
        functools.partial(_ffn_up_kernel, tiles_per_seq=seq // tm),
        grid=(t // tm, nj),
        in_specs=[
            pl.BlockSpec((tm, k), lambda i, j: (i, 0)),
            pl.BlockSpec((SUBLANES, k), lambda i, j: (jnp.maximum(i * hb - 1, 0), 0)),
            pl.BlockSpec((None, k, tn), lambda i, j: (layer, 0, j)),
            pl.BlockSpec((None, k, tn), lambda i, j: (layer, 0, j + nj)),
            pl.BlockSpec((None, CONV_W, tn), lambda i, j: (layer, 0, j)),
            pl.BlockSpec((None, CONV_W, tn), lambda i, j: (layer, 0, j + nj)),
            pl.BlockSpec((None, 1, tn), lambda i, j: (layer, 0, j)),
            pl.BlockSpec((None, 1, tn), lambda i, j: (layer, 0, j + nj)),
        ],
        out_specs=pl.BlockSpec((tm, tn), lambda i, j: (i, j)),
        out_shape=jax.ShapeDtypeStruct((t, f), BF16),
        compiler_params=_cparams("parallel", "arbitrary"),
        name="ffn_up",
    )(h, h, w_up, w_up, conv_w, conv_w, conv_b, conv_b)


def _halfsum(x):
    outs = []
    for gidx in range(x.shape[1] // LANES):
        y = x[:, gidx * LANES:(gidx + 1) * LANES]
        lo = lax.broadcasted_iota(jnp.int32, y.shape, 1) < RWKV_HEAD
        s_lo = jnp.sum(jnp.where(lo, y, 0.0), axis=-1, keepdims=True)
        s_hi = jnp.sum(jnp.where(lo, 0.0, y), axis=-1, keepdims=True)
        outs.append(jnp.where(lo, s_lo, s_hi))
    return jnp.concatenate(outs, axis=1) if len(outs) > 1 else outs[0]


def _rwkv_prep_kernel(*refs, has_vres):
    if has_vres:
        (k_ref, v_ref, hw_ref, ha_ref, hg_ref, w2_ref, a2_ref, g2_ref, vec_ref,
         hv_ref, v2_ref, vf_ref, lw_ref, ko_ref, vo_ref, kk_ref, a_ref, g_ref) = refs
    else:
        (k_ref, v_ref, hw_ref, ha_ref, hg_ref, w2_ref, a2_ref, g2_ref, vec_ref,
         lw_ref, ko_ref, vo_ref, kk_ref, a_ref, g_ref) = refs
    vec = vec_ref[...]
    w0, a0, k_k, k_a, v0 = (vec[n:n + 1, :] for n in range(5))
    k = k_ref[...].astype(F32)
    v = v_ref[...].astype(F32)
    lw_ref[...] = -np.exp(-0.5).astype(np.float32) * jax.nn.sigmoid(w0 + _dot(hw_ref[...], w2_ref[...]))
    a = jax.nn.sigmoid(a0 + _dot(ha_ref[...], a2_ref[...]))
    g_ref[...] = _dot(hg_ref[...], g2_ref[...]).astype(g_ref.dtype)
    if has_vres:
        v = v + (vf_ref[...].astype(F32) - v) * jax.nn.sigmoid(v0 + _dot(hv_ref[...], v2_ref[...]))
    kk = k * k_k
    kk = kk / jnp.maximum(jnp.sqrt(_halfsum(kk * kk)), L2_EPS)
    ko_ref[...] = (k * (1.0 + (a - 1.0) * k_a)).astype(ko_ref.dtype)
    vo_ref[...] = v.astype(vo_ref.dtype)
    kk_ref[...] = kk.astype(kk_ref.dtype)
    a_ref[...] = a.astype(a_ref.dtype)


def _rwkv_prep(k, v, hw, ha, hg, w2, a2, g2, vec, vres, tm=512, tn=512):
    t, d = k.shape
    tm = _pick(t, tm)
    tn = _pick(d, tn)
    tile = pl.BlockSpec((tm, tn), lambda i, j: (i, j))
    rows = lambda arr: pl.BlockSpec((tm, arr.shape[1]), lambda i, j: (i, 0))
    cols = lambda arr: pl.BlockSpec((arr.shape[0], tn), lambda i, j: (0, j))
    args = [k, v, hw, ha, hg, w2, a2, g2, vec]
    in_specs = [tile, tile, rows(hw), rows(ha), rows(hg), cols(w2), cols(a2), cols(g2), cols(vec)]
    if vres is not None:
        hv, v2, v_first = vres
        args += [hv, v2, v_first]
        in_specs += [rows(hv), cols(v2), tile]
    return pl.pallas_call(
        functools.partial(_rwkv_prep_kernel, has_vres=vres is not None),
        grid=(t // tm, d // tn),
        in_specs=in_specs,
        out_specs=[tile] * 6,
        out_shape=[jax.ShapeDtypeStruct((t, d), F32)] + [jax.ShapeDtypeStruct((t, d), BF16)] * 5,
        compiler_params=_cparams("parallel", "parallel"),
        name="rwkv_prep",
    )(*args)


def _pair_masks():
    c = SCAN_CHUNK
    lane = lax.broadcasted_iota(jnp.int32, (c, LANES), 1)
    head0 = lane < RWKV_HEAD
    i = lax.broadcasted_iota(jnp.int32, (2 * c, 2 * c), 0)
    j = lax.broadcasted_iota(jnp.int32, (2 * c, 2 * c), 1)
    same = (i < c) == (j < c)
    ti = jnp.where(i < c, i, i - c)
    tj = jnp.where(j < c, j, j - c)
    row = lax.broadcasted_iota(jnp.int32, (c, c), 0)
    col = lax.broadcasted_iota(jnp.int32, (c, c), 1)
    return dict(head0=head0, strict=same & (ti > tj), incl=same & (ti >= tj), same=same,
                eye=(i == j).astype(F32), tri=(row >= col).astype(BF16))


def _split_heads(x, head0):
    return jnp.concatenate([jnp.where(head0, x, 0.0), jnp.where(head0, 0.0, x)], axis=0)


def _merge_heads(x, head0):
    c = x.shape[0] // 2
    return jnp.where(head0, x[:c], x[c:])


def _twice(x):
    return jnp.concatenate([x, x], axis=0)


def _cumsum_rows(x, tri):
    x0 = x.astype(BF16)
    x1 = (x - x0.astype(F32)).astype(BF16)
    return _dot(tri, x0) + _dot(tri, x1)


def _scan_inv_kernel(lw_ref, kk_ref, a_ref, t_ref, cum_ref, *, chunks, pairs):
    c = SCAN_CHUNK
    mk = _pair_masks()
    head0 = mk["head0"]
    units = [(slice(j * c, (j + 1) * c), slice(pi * LANES, (pi + 1) * LANES))
             for pi in range(pairs) for j in range(chunks)]
    nu = len(units)
    lw = [lw_ref[u] for u in units]
    kk = [kk_ref[u].astype(F32) for u in units]
    cum = [_cumsum_rows(x, mk["tri"]) for x in lw]
    for n, u in enumerate(units):
        cum_ref[u] = cum[n]
    at = [_split_heads(-kk[n] * jnp.exp(cum[n] - lw[n]), head0).astype(BF16) for n in range(nu)]
    bt = [_twice(kk[n] * a_ref[u].astype(F32) * jnp.exp(-cum[n])).astype(BF16) for n, u in enumerate(units)]
    ab = [jnp.where(mk["strict"], _dot_nt(at[n], bt[n]), 0.0) for n in range(nu)]
    t = [mk["eye"] + x for x in ab]
    pw = [x.astype(BF16) for x in ab]
    pw = [_dot(x, x).astype(BF16) for x in pw]
    doublings = int(np.log2(c)) - 1
    for it in range(doublings):
        t = [t[n] + _dot(t[n].astype(BF16), pw[n]) for n in range(nu)]
        if it < doublings - 1:
            pw = [_dot(x, x).astype(BF16) for x in pw]
    for n, u in enumerate(units):
        t_ref[u] = (t[n][:c] + t[n][c:]).astype(t_ref.dtype)


def _scan_kernel(r_ref, cum_ref, lw_ref, k_ref, v_ref, kk_ref, a_ref, g_ref, t_ref, gnw_ref, gnb_ref, rk_ref,
                 o_ref, s_ref, *, chunks, pairs):
    c = SCAN_CHUNK
    hd = RWKV_HEAD

    @pl.when(pl.program_id(2) == 0)
    def _():
        s_ref[...] = jnp.zeros_like(s_ref)

    mk = _pair_masks()
    head0 = mk["head0"]
    lanes = [slice(pi * LANES, (pi + 1) * LANES) for pi in range(pairs)]

    def per_head(x):
        s0 = jnp.sum(jnp.where(head0, x, 0.0), axis=-1, keepdims=True)
        s1 = jnp.sum(jnp.where(head0, 0.0, x), axis=-1, keepdims=True)
        return jnp.where(head0, s0, s1)

    def prepare(j):
        rows = slice(j * c, (j + 1) * c)
        pre = []
        for ls in lanes:
            u = (rows, ls)
            cum = cum_ref[u]
            k = k_ref[u].astype(F32)
            kk = kk_ref[u].astype(F32)
            b = kk * a_ref[u].astype(F32)
            at = -kk * jnp.exp(cum - lw_ref[u])
            rt = r_ref[u].astype(F32) * jnp.exp(cum)
            pinv = jnp.exp(-cum)
            to_end = jnp.exp(cum[c - 1:c, :] - cum)
            lhs = jnp.concatenate([_split_heads(at, head0), _split_heads(rt, head0)], axis=0).astype(BF16)
            pre.append(dict(
                u=u, lhs=lhs, v16=v_ref[u], p_end=jnp.exp(cum[c - 1:c, :]),
                kt=_twice(k * pinv).astype(BF16), bt=_twice(b * pinv).astype(BF16),
                ar=jnp.concatenate([at, rt], axis=0).astype(BF16),
                kb_end=jnp.concatenate([(k * to_end).astype(BF16), (b * to_end).astype(BF16)], axis=0),
                tinv=_split_heads(t_ref[u].astype(F32), head0).astype(BF16)))
        return pre

    def scores_k(pre):
        for q in pre:
            q["sk"] = _dot_nt(q["lhs"], q["kt"])

    def scores_b(pre):
        for q in pre:
            sb = _dot_nt(q["lhs"][2 * c:], q["bt"])
            ak = jnp.where(mk["strict"], q["sk"][:2 * c], 0.0).astype(BF16)
            q["akv"] = _merge_heads(_dot(ak, _twice(q["v16"])), head0)
            q["rkb"] = jnp.concatenate([jnp.where(mk["incl"], q["sk"][2 * c:], 0.0),
                                        jnp.where(mk["incl"], sb, 0.0)], axis=1).astype(BF16)

    def finish(pre):
        for q in pre:
            u = q["u"]
            ls = u[1]
            vu2 = jnp.concatenate([_twice(q["v16"]), _twice(q["u16"])], axis=0)
            y = q["zs_r"] + _merge_heads(_dot(q["rkb"], vu2), head0)
            mean = per_head(y) * (1.0 / hd)
            yc = y - mean
            var = per_head(yc * yc) * (1.0 / hd)
            yn = yc * lax.rsqrt(var + GN_EPS) * gnw_ref[:, ls] + gnb_ref[:, ls]
            bonus = per_head(r_ref[u].astype(F32) * k_ref[u].astype(F32) * rk_ref[:, ls]) * v_ref[u].astype(F32)
            o_ref[u] = ((yn + bonus) * g_ref[u].astype(F32)).astype(o_ref.dtype)

    s = [s_ref[pi] for pi in range(pairs)]
    cur = prepare(0)
    scores_k(cur)
    scores_b(cur)
    prev = None
    for j in range(chunks):
        nxt = prepare(j + 1) if j + 1 < chunks else None
        zs = [_dot_nt(q["ar"], s[pi].astype(BF16)) for pi, q in enumerate(cur)]
        if nxt is not None:
            scores_k(nxt)
        for pi, q in enumerate(cur):
            z16 = (zs[pi][:c] + q["akv"]).astype(BF16)
            q["u16"] = _merge_heads(_dot(q["tinv"], _twice(z16)), head0).astype(BF16)
            q["zs_r"] = zs[pi][c:]
        if nxt is not None:
            scores_b(nxt)
        for pi, q in enumerate(cur):
            vu = jnp.concatenate([q["v16"], q["u16"]], axis=0)
            s[pi] = s[pi] * q["p_end"] + jnp.where(mk["same"], _dot_tn(vu, q["kb_end"]), 0.0)
        if prev is not None:
            finish(prev)
        prev, cur = cur, nxt
    finish(prev)
    for pi in range(pairs):
        s_ref[pi] = s[pi]


def _rwkv_scan(r, lw, k, v, kk, a, g, gn_w, gn_b, r_k, batch, seq, tc=256, pairs=8, tc_inv=128):
    t, d = r.shape
    tc = _pick(seq, tc)
    tc_inv = _pick(t, tc_inv)
    pairs = min(pairs, d // LANES)
    w = pairs * LANES
    itile = pl.BlockSpec((tc_inv, w), lambda i, pi: (i, pi))
    tinv, cum = pl.pallas_call(
        functools.partial(_scan_inv_kernel, chunks=tc_inv // SCAN_CHUNK, pairs=pairs),
        grid=(t // tc_inv, d // w),
        in_specs=[itile] * 3,
        out_specs=[itile] * 2,
        out_shape=[jax.ShapeDtypeStruct((t, d), BF16), jax.ShapeDtypeStruct((t, d), F32)],
        compiler_params=_cparams("parallel", "parallel"),
        name="rwkv_inv",
    )(lw, kk, a)
    steps = seq // tc
    tile = pl.BlockSpec((tc, w), lambda bi, pi, ci: (bi * steps + ci, pi))
    vec = pl.BlockSpec((1, w), lambda bi, pi, ci: (0, pi))
    return pl.pallas_call(
        functools.partial(_scan_kernel, chunks=tc // SCAN_CHUNK, pairs=pairs),
        grid=(batch, d // w, steps),
        in_specs=[tile] * 9 + [vec] * 3,
        out_specs=tile,
        out_shape=jax.ShapeDtypeStruct((t, d), BF16),
        scratch_shapes=[pltpu.VMEM((pairs, LANES, LANES), F32)],
        compiler_params=_cparams("parallel", "parallel", "arbitrary"),
        name="rwkv_scan",
    )(r, cum, lw, k, v, kk, a, g, tinv, gn_w.reshape(1, d), gn_b.reshape(1, d), r_k.reshape(1, d))


SPAN = BLOCK * BRANCHES[-1][1]
ATT_HPS = 2
UNITS_PER_ITER = 4


def _attn_kernel(*refs):
    nbr = len(BRANCHES)
    q_refs = [refs[i * ATT_HPS:(i + 1) * ATT_HPS] for i in range(nbr)]
    base = nbr * ATT_HPS
    kc_refs, vc_refs, kp_refs, vp_refs, bias_refs = (refs[base + j * nbr: base + (j + 1) * nbr] for j in range(5))
    o_ref, acc_ref, m_ref, l_ref, stage_ref = refs[base + 5 * nbr:]
    blk = BLOCK
    n = pl.program_id(1)
    scale = ATT_HEAD ** -0.5
    kj = lax.broadcasted_iota(jnp.int32, (ATT_HPS * blk, 2 * blk), 1)
    widest = nbr - 1
    d_max = BRANCHES[widest][1]
    sub = 4
    assert d_max == sub * sub and BRANCHES[1][1] == sub and BRANCHES[0][1] == 1
    run = SPAN // sub

    staged = [q_refs[widest][e] for e in range(ATT_HPS)] + [kc_refs[widest], vc_refs[widest],
                                                           kp_refs[widest], vp_refs[widest]]
    for a, src in enumerate(staged):
        for r4 in range(sub):
            stage_ref[a, r4 * run:(r4 + 1) * run, :] = src[pl.ds(r4, run, stride=sub), :]

    def load_unit(i, d, u):
        if i == widest:
            r_lo = u % sub
            off = r_lo * run + u // sub
            rows_s = pl.ds(off, blk, stride=sub)
            q = jnp.concatenate([stage_ref[e, rows_s, :] for e in range(ATT_HPS)], axis=0)
            kcat = jnp.concatenate([stage_ref[ATT_HPS + 2, rows_s, :], stage_ref[ATT_HPS, rows_s, :]], axis=0)
            vcat = jnp.concatenate([stage_ref[ATT_HPS + 3, rows_s, :], stage_ref[ATT_HPS + 1, rows_s, :]], axis=0)
            return q, kcat, vcat, pl.ds(u, blk, stride=d), True
        nb = u // d
        start = nb * (blk * d) + (u - nb * d)
        rows = pl.ds(start, blk, stride=d)
        q = jnp.concatenate([q_refs[i][e][rows, :] for e in range(ATT_HPS)], axis=0)
        prow_in_cur = pl.ds(jnp.maximum(start - blk * d, 0), blk, stride=d)
        prow_in_prev = pl.ds(u - nb * d, blk, stride=d)
        first = nb == 0
        kprev = jnp.where(first, kp_refs[i][prow_in_prev, :], kc_refs[i][prow_in_cur, :])
        vprev = jnp.where(first, vp_refs[i][prow_in_prev, :], vc_refs[i][prow_in_cur, :])
        kcat = jnp.concatenate([kprev, kc_refs[i][rows, :]], axis=0)
        vcat = jnp.concatenate([vprev, vc_refs[i][rows, :]], axis=0)
        return q, kcat, vcat, rows, first

    for order, i in enumerate(reversed(range(nbr))):
        d = BRANCHES[i][1]

        def body(it, carry, i=i, d=d, init=order == 0):
            loaded = [load_unit(i, d, it * UNITS_PER_ITER + x) for x in range(UNITS_PER_ITER)]
            bias = bias_refs[i][...].reshape(ATT_HPS * blk, 2 * blk)
            ss = []
            for q, kcat, _, _, first in loaded:
                s = _dot_nt((q * scale).astype(BF16), kcat.astype(BF16)) + bias
                ss.append(jnp.where((n == 0) & first & (kj < blk), -jnp.inf, s))
            ms = [jnp.max(s, axis=-1, keepdims=True) for s in ss]
            ps = [jnp.exp(s - m) for s, m in zip(ss, ms)]
            ls = [jnp.sum(p, axis=-1, keepdims=True) for p in ps]
            os_ = [_dot(p.astype(BF16), ld[2].astype(BF16)) for p, ld in zip(ps, loaded)]
            for x in range(UNITS_PER_ITER):
                rows = loaded[x][3]
                for e in range(ATT_HPS):
                    hr = slice(e * blk, (e + 1) * blk)
                    m_new = jnp.broadcast_to(ms[x][hr], (blk, ATT_HEAD))
                    l_new = jnp.broadcast_to(ls[x][hr], (blk, ATT_HEAD))
                    o_new = os_[x][hr]
                    if init:
                        m_ref[e, rows, :] = m_new
                        l_ref[e, rows, :] = l_new
                        acc_ref[e, rows, :] = o_new
                    else:
                        m_old = m_ref[e, rows, :]
                        m2 = jnp.maximum(m_old, m_new)
                        w_old = jnp.exp(m_old - m2)
                        w_new = jnp.exp(m_new - m2)
                        m_ref[e, rows, :] = m2
                        l_ref[e, rows, :] = l_ref[e, rows, :] * w_old + l_new * w_new
                        acc_ref[e, rows, :] = acc_ref[e, rows, :] * w_old + o_new * w_new
            return carry

        lax.fori_loop(0, (SPAN // blk) // UNITS_PER_ITER, body, 0)

    for e in range(ATT_HPS):
        o_ref[:, e * ATT_HEAD:(e + 1) * ATT_HEAD] = (acc_ref[e] / l_ref[e]).astype(o_ref.dtype)


def _attention(q, kv, biases, batch, seq, heads, groups):
    t = q.shape[0]
    nsb = seq // SPAN
    per_kv = heads // groups
    steps = heads // ATT_HPS
    in_specs, args = [], []
    for i in range(N_BR):
        for e in range(ATT_HPS):
            in_specs.append(pl.BlockSpec((SPAN, ATT_HEAD),
                                         lambda b, n, hp, i=i, e=e: (b * nsb + n, i * heads + hp * ATT_HPS + e)))
            args.append(q)
    kvcol = lambda i, sel, hp: (i * 2 + sel) * groups + (hp * ATT_HPS) // per_kv
    for sel in range(2):
        for i in range(N_BR):
            in_specs.append(pl.BlockSpec((SPAN, ATT_HEAD),
                                         lambda b, n, hp, i=i, sel=sel: (b * nsb + n, kvcol(i, sel, hp))))
            args.append(kv)
    for sel in range(2):
        for i, (_, d) in enumerate(BRANCHES):
            rows = BLOCK * d
            in_specs.append(pl.BlockSpec(
                (rows, ATT_HEAD),
                lambda b, n, hp, i=i, sel=sel, rows=rows: (jnp.maximum((b * nsb + n) * (SPAN // rows) - 1, 0),
                                                            kvcol(i, sel, hp))))
            args.append(kv)
    for i in range(N_BR):
        in_specs.append(pl.BlockSpec((ATT_HPS, BLOCK, 2 * BLOCK), lambda b, n, hp: (hp, 0, 0)))
        args.append(biases[i])
    return pl.pallas_call(
        _attn_kernel,
        grid=(batch, nsb, steps),
        in_specs=in_specs,
        out_specs=pl.BlockSpec((SPAN, ATT_HPS * ATT_HEAD), lambda b, n, hp: (b * nsb + n, hp)),
        out_shape=jax.ShapeDtypeStruct((t, heads * ATT_HEAD), BF16),
        scratch_shapes=[pltpu.VMEM((ATT_HPS, SPAN, ATT_HEAD), F32)] * 3
        + [pltpu.VMEM((ATT_HPS + 4, SPAN, ATT_HEAD), F32)],
        compiler_params=_cparams("parallel", "parallel", "arbitrary"),
        name="dilated_attn",
    )(*args)


def _t5_bucket(distance):
    max_exact = REL_BUCKETS // 2
    dist = np.asarray(distance, dtype=np.int64)
    scaled = np.log(np.maximum(dist, max_exact) / max_exact) / np.log(REL_MAX_DIST / max_exact)
    large = np.minimum(max_exact + (scaled * (REL_BUCKETS - max_exact)).astype(np.int64), REL_BUCKETS - 1)
    return np.where(dist < max_exact, dist, large).astype(np.int32)


def _bias_table(rel_bias, window, dilation):
    band = window // dilation
    qi = np.arange(BLOCK)[:, None]
    kj = np.arange(2 * BLOCK)[None, :]
    rel = qi + BLOCK - kj
    bucket = _t5_bucket(np.clip(rel, 0, band) * dilation)
    bias = jnp.transpose(rel_bias[bucket], (2, 0, 1)).astype(F32)
    return jnp.where(((rel >= 0) & (rel <= band))[None], bias, -jnp.inf)


def _conv_ffn_layer(x, norm_g, w_up, conv_w, conv_b, w_down, layer, seq):
    h = _rmsnorm(x, norm_g)
    act = _ffn_up(h, w_up, conv_w, conv_b[:, None, :], layer, seq)
    return _matmul(act, w_down, F32, layer=layer, epilogue="residual", extra=(x,), tm=512, tn=256)


def _rwkv_layer(x, v_first, layer, norm_g, mu, w0, w1, w2, a0, a1, a2, vres_w, g1, g2, k_k, k_a, r_k,
                w_r, w_k, w_v, w_o, gn_w, gn_b, batch, seq):
    d = x.shape[1]
    xr, xw, xk, xv, xa, xg = _rms_mix(x, norm_g, mu, seq)
    r = _matmul(xr, w_r, BF16, layer=layer)
    k = _matmul(xk, w_k, BF16, layer=layer)
    v = _matmul(xv, w_v, BF16, layer=layer)
    hw = _matmul(xw, w1, BF16, layer=layer, epilogue="tanh")
    ha = _matmul(xa, a1, BF16, layer=layer)
    hg = _matmul(xg, g1, BF16, layer=layer, epilogue="sigmoid")
    zeros = jnp.zeros((d,), F32)
    if vres_w is None:
        vec = jnp.stack([w0, a0, k_k, k_a, zeros, zeros, zeros, zeros])
        vres = None
    else:
        v0, v1, v2 = vres_w
        vec = jnp.stack([w0, a0, k_k, k_a, v0, zeros, zeros, zeros])
        vres = (_matmul(xv, v1, BF16), v2.astype(BF16), v_first)
    lw, k2, v2_, kk, a, g = _rwkv_prep(k, v, hw, ha, hg, w2.astype(BF16), a2.astype(BF16), g2.astype(BF16),
                                       vec, vres)
    if vres_w is None:
        v_first = v2_
    out = _rwkv_scan(r, lw, k2, v2_, kk, a, g, gn_w, gn_b, r_k, batch, seq)
    return _matmul(out, w_o, F32, layer=layer, epilogue="residual", extra=(x,)), v_first


def _attn_layer(x, h, kv, w_q, q_gain, w_o, j, biases, batch, seq, heads, groups):
    gain = jnp.broadcast_to(q_gain[:, None, :], (N_BR, heads, ATT_HEAD)).reshape(1, -1)
    q = _matmul(h, w_q, F32, layer=j, epilogue="headnorm", extra=(gain, jnp.ones_like(gain)))
    o = _attention(q, kv, biases, batch, seq, heads, groups)
    return _matmul(o, w_o, F32, layer=j, epilogue="residual", extra=(x,))


def kernel(x, norm_mix, norm_ffn, rwkv_mu, rwkv_w0, rwkv_w1, rwkv_w2, rwkv_a0, rwkv_a1, rwkv_a2, rwkv_v0, rwkv_v1, rwkv_v2, rwkv_g1, rwkv_g2, rwkv_k_k, rwkv_k_a, rwkv_r_k, rwkv_w_r, rwkv_w_k, rwkv_w_v, rwkv_w_o, rwkv_gn_w, rwkv_gn_b, norm_kv, attn_w_kv, attn_k_gain, attn_w_q, attn_q_gain, attn_w_o, rel_bias, ffn_w_up, ffn_conv_w, ffn_conv_b, ffn_w_down):
    batch, seq, d = x.shape
    depth = norm_mix.shape[0]
    n_a = rwkv_mu.shape[0]
    heads = attn_w_o.shape[-2] // ATT_HEAD
    groups = attn_w_kv.shape[-1] // (N_BR * 2 * ATT_HEAD)
    assert seq % SPAN == 0 and d % LANES == 0 and heads % groups == 0 and (heads // groups) % ATT_HPS == 0
    x = x.reshape(batch * seq, d)
    v_first = None
    kv = None
    biases = None
    w_down16 = ffn_w_down.astype(BF16)
    for layer in range(depth):
        if layer < n_a:
            vres_w = None if layer == 0 else (rwkv_v0[layer - 1], rwkv_v1[layer - 1], rwkv_v2[layer - 1])
            x, v_first = _rwkv_layer(
                x, v_first, layer, norm_mix[layer], rwkv_mu[layer], rwkv_w0[layer], rwkv_w1, rwkv_w2[layer],
                rwkv_a0[layer], rwkv_a1, rwkv_a2[layer], vres_w, rwkv_g1, rwkv_g2[layer], rwkv_k_k[layer],
                rwkv_k_a[layer], rwkv_r_k[layer], rwkv_w_r, rwkv_w_k, rwkv_w_v, rwkv_w_o, rwkv_gn_w[layer],
                rwkv_gn_b[layer], batch, seq)
        else:
            if layer == n_a:
                k_gain = jnp.broadcast_to(attn_k_gain[:, None, None, :], (N_BR, 2, groups, ATT_HEAD))
                k_flag = jnp.broadcast_to(jnp.array([1.0, 0.0], F32)[None, :, None, None],
                                          (N_BR, 2, groups, ATT_HEAD))
                kv = _matmul(_rmsnorm(x, norm_kv), attn_w_kv, F32, epilogue="headnorm",
                             extra=(k_gain.reshape(1, -1), k_flag.reshape(1, -1)))
                biases = [_bias_table(rel_bias, w, dl) for w, dl in BRANCHES]
            j = layer - n_a
            h = _rmsnorm(x, norm_mix[layer])
            x = _attn_layer(x, h, kv, attn_w_q, attn_q_gain[j], attn_w_o, j, biases, batch, seq, heads, groups)
        x = _conv_ffn_layer(x, norm_ffn[layer], ffn_w_up, ffn_conv_w, ffn_conv_b, w_down16, layer, seq)
    return x.reshape(batch, seq, d)
```

```python
import functools

import numpy as np
import jax
import jax.numpy as jnp
from jax import lax
from jax.experimental import pallas as pl
from jax.experimental.pallas import tpu as pltpu

F32 = jnp.float32
BF16 = jnp.bfloat16

RWKV_HEAD = 64
ATT_HEAD = 128
N_BR = 3
BRANCHES = ((128, 1), (512, 4), (2048, 16))
BLOCK = 128
REL_BUCKETS = 32
REL_MAX_DIST = 2048
CONV_W = 3
NORM_EPS = 1e-6
GN_EPS = 64e-5
L2_EPS = 1e-12

LANES = 128
SUBLANES = 8
VMEM_LIMIT_BYTES = 56 * 1024 * 1024

SCAN_CHUNK = 64


def _cparams(*sem):
    return pltpu.CompilerParams(dimension_semantics=sem, vmem_limit_bytes=VMEM_LIMIT_BYTES)


def _pick(n, pref):
    t = min(pref, n)
    while n % t:
        t //= 2
    return t


def _rms(x, g):
    return x * lax.rsqrt(jnp.mean(x * x, axis=-1, keepdims=True) + NORM_EPS) * g


def _dot(a, b):
    return jnp.dot(a, b, preferred_element_type=F32)


def _dot_nt(a, b):
    return lax.dot_general(a, b, (((1,), (1,)), ((), ())), preferred_element_type=F32)


def _dot_tn(a, b):
    return lax.dot_general(a, b, (((0,), (0,)), ((), ())), preferred_element_type=F32)


def _shift_rows(u, halo, n):
    rolled = pltpu.roll(u, n, 0)
    row = lax.broadcasted_iota(jnp.int32, u.shape, 0)
    for r in range(n):
        rolled = jnp.where(row == r, halo[SUBLANES - n + r:SUBLANES - n + r + 1, :], rolled)
    return rolled


def _rms_kernel(x_ref, g_ref, o_ref):
    o_ref[...] = _rms(x_ref[...], g_ref[...]).astype(o_ref.dtype)


def _rmsnorm(x, g, tm=256):
    t, d = x.shape
    tm = _pick(t, tm)
    return pl.pallas_call(
        _rms_kernel,
        grid=(t // tm,),
        in_specs=[pl.BlockSpec((tm, d), lambda i: (i, 0)), pl.BlockSpec((1, d), lambda i: (0, 0))],
        out_specs=pl.BlockSpec((tm, d), lambda i: (i, 0)),
        out_shape=jax.ShapeDtypeStruct((t, d), BF16),
        compiler_params=_cparams("parallel"),
        name="rmsnorm",
    )(x, g.reshape(1, d))


def _rms_mix_kernel(x_ref, halo_ref, g_ref, mu_ref, *o_refs, tiles_per_seq):
    i = pl.program_id(0)
    g = g_ref[...]
    h = _rms(x_ref[...], g)
    hh = _rms(halo_ref[...], g)
    hh = jnp.where(i % tiles_per_seq == 0, 0.0, hh)
    xx = _shift_rows(h, hh, 1) - h
    for n, o_ref in enumerate(o_refs):
        o_ref[...] = (h + xx * mu_ref[n:n + 1, :]).astype(o_ref.dtype)


def _rms_mix(x, g, mu, seq, tm=256):
    t, d = x.shape
    tm = _pick(seq, tm)
    nmix = mu.shape[0]
    hb = tm // SUBLANES
    return pl.pallas_call(
        functools.partial(_rms_mix_kernel, tiles_per_seq=seq // tm),
        grid=(t // tm,),
        in_specs=[
            pl.BlockSpec((tm, d), lambda i: (i, 0)),
            pl.BlockSpec((SUBLANES, d), lambda i: (jnp.maximum(i * hb - 1, 0), 0)),
            pl.BlockSpec((1, d), lambda i: (0, 0)),
            pl.BlockSpec((nmix, d), lambda i: (0, 0)),
        ],
        out_specs=[pl.BlockSpec((tm, d), lambda i: (i, 0))] * nmix,
        out_shape=[jax.ShapeDtypeStruct((t, d), BF16)] * nmix,
        compiler_params=_cparams("parallel"),
        name="rms_mix",
    )(x, x, g.reshape(1, d), mu)


def _headnorm(acc, gain, flag):
    outs = []
    for h in range(acc.shape[1] // ATT_HEAD):
        sl = slice(h * ATT_HEAD, (h + 1) * ATT_HEAD)
        y = acc[:, sl]
        outs.append(jnp.where(flag[:, sl] != 0.0, _rms(y, gain[:, sl]), y))
    return jnp.concatenate(outs, axis=1) if len(outs) > 1 else outs[0]


def _mm_kernel(a_ref, w_ref, *rest, epilogue):
    o_ref = rest[-1]
    acc = _dot(a_ref[...], w_ref[...].astype(BF16))
    if epilogue == "tanh":
        acc = jnp.tanh(acc)
    elif epilogue == "sigmoid":
        acc = jax.nn.sigmoid(acc)
    elif epilogue == "residual":
        acc = rest[0][...] + acc
    elif epilogue == "headnorm":
        acc = _headnorm(acc, rest[0][...], rest[1][...])
    o_ref[...] = acc.astype(o_ref.dtype)


def _matmul(a, w, out_dtype, *, layer=None, epilogue="none", extra=(), tm=1024, tn=512):
    m, k = a.shape
    n = w.shape[-1]
    tm = _pick(m, tm)
    tn = n if n % LANES else _pick(n, tn)
    if layer is None:
        w_spec = pl.BlockSpec((k, tn), lambda i, j: (0, j))
    else:
        w_spec = pl.BlockSpec((None, k, tn), lambda i, j: (layer, 0, j))
    in_specs = [pl.BlockSpec((tm, k), lambda i, j: (i, 0)), w_spec]
    for e in extra:
        if e.shape[0] == 1:
            in_specs.append(pl.BlockSpec((1, tn), lambda i, j: (0, j)))
        else:
            in_specs.append(pl.BlockSpec((tm, tn), lambda i, j: (i, j)))
    return pl.pallas_call(
        functools.partial(_mm_kernel, epilogue=epilogue),
        grid=(m // tm, n // tn),
        in_specs=in_specs,
        out_specs=pl.BlockSpec((tm, tn), lambda i, j: (i, j)),
        out_shape=jax.ShapeDtypeStruct((m, n), out_dtype),
        compiler_params=_cparams("parallel", "arbitrary"),
        name="matmul_" + epilogue,
    )(a, w, *extra)


def _ffn_up_kernel(h_ref, halo_ref, wg_ref, wv_ref, cwg_ref, cwv_ref, cbg_ref, cbv_ref, o_ref, *, tiles_per_seq):
    i = pl.program_id(0)
    first = i % tiles_per_seq == 0
    h = h_ref[...]
    halo = halo_ref[...]

    def conv(w_ref, cw_ref, cb_ref):
        w = w_ref[...].astype(BF16)
        u = _dot(h, w)
        uh = jnp.where(first, 0.0, _dot(halo, w))
        cw = cw_ref[...]
        return (cw[2:3, :] * u + cw[1:2, :] * _shift_rows(u, uh, 1)
                + cw[0:1, :] * _shift_rows(u, uh, 2) + cb_ref[...])

    gate = conv(wg_ref, cwg_ref, cbg_ref)
    val = conv(wv_ref, cwv_ref, cbv_ref)
    o_ref[...] = (jax.nn.silu(gate) * val).astype(o_ref.dtype)


def _ffn_up(h, w_up, conv_w, conv_b, layer, seq, tm=1024, tn=256):
    t, k = h.shape
    f = w_up.shape[-1] // 2
    tm = _pick(seq, tm)
    tn = _pick(f, tn)
    nj = f // tn
    hb = tm // SUBLANES
    return pl.pallas_call(
        functools.partial(_ffn_up_kernel, tiles_per_seq=seq // tm),
        grid=(t // tm, nj),
        in_specs=[
            pl.BlockSpec((tm, k), lambda i, j: (i, 0)),
            pl.BlockSpec((SUBLANES, k), lambda i, j: (jnp.maximum(i * hb - 1, 0), 0)),
            pl.BlockSpec((None, k, tn), lambda i, j: (layer, 0, j)),
            pl.BlockSpec((None, k, tn), lambda i, j: (layer, 0, j + nj)),
            pl.BlockSpec((None, CONV_W, tn), lambda i, j: (layer, 0, j)),
            pl.BlockSpec((None, CONV_W, tn), lambda i, j: (layer, 0, j + nj)),
            pl.BlockSpec((None, 1, tn), lambda i, j: (layer, 0, j)),
            pl.BlockSpec((None, 1, tn), lambda i, j: (layer, 0, j + nj)),
        ],
        out_specs=pl.BlockSpec((tm, tn), lambda i, j: (i, j)),
        out_shape=jax.ShapeDtypeStruct((t, f), BF16),
        compiler_params=_cparams("parallel", "arbitrary"),
        name="ffn_up",
    )(h, h, w_up, w_up, conv_w, conv_w, conv_b, conv_b)


def _halfsum(x):
    outs = []
    for gidx in range(x.shape[1] // LANES):
        y = x[:, gidx * LANES:(gidx + 1) * LANES]
        lo = lax.broadcasted_iota(jnp.int32, y.shape, 1) < RWKV_HEAD
        s_lo = jnp.sum(jnp.where(lo, y, 0.0), axis=-1, keepdims=True)
        s_hi = jnp.sum(jnp.where(lo, 0.0, y), axis=-1, keepdims=True)
        outs.append(jnp.where(lo, s_lo, s_hi))
    return jnp.concatenate(outs, axis=1) if len(outs) > 1 else outs[0]


def _rwkv_prep_kernel(*refs, has_vres):
    if has_vres:
        (k_ref, v_ref, hw_ref, ha_ref, hg_ref, w2_ref, a2_ref, g2_ref, vec_ref,
         hv_ref, v2_ref, vf_ref, lw_ref, ko_ref, vo_ref, kk_ref, a_ref, g_ref) = refs
    else:
        (k_ref, v_ref, hw_ref, ha_ref, hg_ref, w2_ref, a2_ref, g2_ref, vec_ref,
         lw_ref, ko_ref, vo_ref, kk_ref, a_ref, g_ref) = refs
    vec = vec_ref[...]
    w0, a0, k_k, k_a, v0 = (vec[n:n + 1, :] for n in range(5))
    k = k_ref[...].astype(F32)
    v = v_ref[...].astype(F32)
    lw_ref[...] = -np.exp(-0.5).astype(np.float32) * jax.nn.sigmoid(w0 + _dot(hw_ref[...], w2_ref[...]))
    a = jax.nn.sigmoid(a0 + _dot(ha_ref[...], a2_ref[...]))
    g_ref[...] = _dot(hg_ref[...], g2_ref[...]).astype(g_ref.dtype)
    if has_vres:
        v = v + (vf_ref[...].astype(F32) - v) * jax.nn.sigmoid(v0 + _dot(hv_ref[...], v2_ref[...]))
    kk = k * k_k
    kk = kk / jnp.maximum(jnp.sqrt(_halfsum(kk * kk)), L2_EPS)
    ko_ref[...] = (k * (1.0 + (a - 1.0) * k_a)).astype(ko_ref.dtype)
    vo_ref[...] = v.astype(vo_ref.dtype)
    kk_ref[...] = kk.astype(kk_ref.dtype)
    a_ref[...] = a.astype(a_ref.dtype)


def _rwkv_prep(k, v, hw, ha, hg, w2, a2, g2, vec, vres, tm=512, tn=512):
    t, d = k.shape
    tm = _pick(t, tm)
    tn = _pick(d, tn)
    tile = pl.BlockSpec((tm, tn), lambda i, j: (i, j))
    rows = lambda arr: pl.BlockSpec((tm, arr.shape[1]), lambda i, j: (i, 0))
    cols = lambda arr: pl.BlockSpec((arr.shape[0], tn), lambda i, j: (0, j))
    args = [k, v, hw, ha, hg, w2, a2, g2, vec]
    in_specs = [tile, tile, rows(hw), rows(ha), rows(hg), cols(w2), cols(a2), cols(g2), cols(vec)]
    if vres is not None:
        hv, v2, v_first = vres
        args += [hv, v2, v_first]
        in_specs += [rows(hv), cols(v2), tile]
    return pl.pallas_call(
        functools.partial(_rwkv_prep_kernel, has_vres=vres is not None),
        grid=(t // tm, d // tn),
        in_specs=in_specs,
        out_specs=[tile] * 6,
        out_shape=[jax.ShapeDtypeStruct((t, d), F32)] + [jax.ShapeDtypeStruct((t, d), BF16)] * 5,
        compiler_params=_cparams("parallel", "parallel"),
        name="rwkv_prep",
    )(*args)


def _pair_masks():
    c = SCAN_CHUNK
    lane = lax.broadcasted_iota(jnp.int32, (c, LANES), 1)
    head0 = lane < RWKV_HEAD
    i = lax.broadcasted_iota(jnp.int32, (2 * c, 2 * c), 0)
    j = lax.broadcasted_iota(jnp.int32, (2 * c, 2 * c), 1)
    same = (i < c) == (j < c)
    ti = jnp.where(i < c, i, i - c)
    tj = jnp.where(j < c, j, j - c)
    row = lax.broadcasted_iota(jnp.int32, (c, c), 0)
    col = lax.broadcasted_iota(jnp.int32, (c, c), 1)
    return dict(head0=head0, strict=same & (ti > tj), incl=same & (ti >= tj), same=same,
                eye=(i == j).astype(F32), tri=(row >= col).astype(BF16))


def _split_heads(x, head0):
    return jnp.concatenate([jnp.where(head0, x, 0.0), jnp.where(head0, 0.0, x)], axis=0)


def _merge_heads(x, head0):
    c = x.shape[0] // 2
    return jnp.where(head0, x[:c], x[c:])


def _twice(x):
    return jnp.concatenate([x, x], axis=0)


def _cumsum_rows(x, tri):
    x0 = x.astype(BF16)
    x1 = (x - x0.astype(F32)).astype(BF16)
    return _dot(tri, x0) + _dot(tri, x1)


def _scan_inv_kernel(lw_ref, kk_ref, a_ref, t_ref, cum_ref, *, chunks, pairs):
    c = SCAN_CHUNK
    mk = _pair_masks()
    head0 = mk["head0"]
    units = [(slice(j * c, (j + 1) * c), slice(pi * LANES, (pi + 1) * LANES))
             for pi in range(pairs) for j in range(chunks)]
    nu = len(units)
    lw = [lw_ref[u] for u in units]
    kk = [kk_ref[u].astype(F32) for u in units]
    cum = [_cumsum_rows(x, mk["tri"]) for x in lw]
    for n, u in enumerate(units):
        cum_ref[u] = cum[n]
    at = [_split_heads(-kk[n] * jnp.exp(cum[n] - lw[n]), head0).astype(BF16) for n in range(nu)]
    bt = [_twice(kk[n] * a_ref[u].astype(F32) * jnp.exp(-cum[n])).astype(BF16) for n, u in enumerate(units)]
    ab = [jnp.where(mk["strict"], _dot_nt(at[n], bt[n]), 0.0) for n in range(nu)]
    i = lax.broadcasted_iota(jnp.int32, (2 * c, 2 * c), 0)
    j = lax.broadcasted_iota(jnp.int32, (2 * c, 2 * c), 1)
    t = None
    s = 1
    while s < c:
        lower_left = ((i // (2 * s)) == (j // (2 * s))) & ((i // s) % 2 == 1) & ((j // s) % 2 == 0)
        q = [jnp.where(lower_left, x, 0.0) for x in ab]
        if t is None:
            t = [mk["eye"] + x for x in q]
        else:
            qt = [_dot(q[n].astype(BF16), t[n].astype(BF16)) for n in range(nu)]
            t = [t[n] + _dot(t[n].astype(BF16), qt[n].astype(BF16)) for n in range(nu)]
        s *= 2
    for n, u in enumerate(units):
        t_ref[u] = (t[n][:c] + t[n][c:]).astype(t_ref.dtype)


def _scan_kernel(r_ref, cum_ref, lw_ref, k_ref, v_ref, kk_ref, a_ref, g_ref, t_ref, gnw_ref, gnb_ref, rk_ref,
                 o_ref, s_ref, *, chunks, pairs):
    c = SCAN_CHUNK
    hd = RWKV_HEAD

    @pl.when(pl.program_id(2) == 0)
    def _():
        s_ref[...] = jnp.zeros_like(s_ref)

    mk = _pair_masks()
    head0 = mk["head0"]
    lanes = [slice(pi * LANES, (pi + 1) * LANES) for pi in range(pairs)]

    def per_head(x):
        s0 = jnp.sum(jnp.where(head0, x, 0.0), axis=-1, keepdims=True)
        s1 = jnp.sum(jnp.where(head0, 0.0, x), axis=-1, keepdims=True)
        return jnp.where(head0, s0, s1)

    def prepare(j):
        rows = slice(j * c, (j + 1) * c)
        pre = []
        for ls in lanes:
            u = (rows, ls)
            cum = cum_ref[u]
            k = k_ref[u].astype(F32)
            kk = kk_ref[u].astype(F32)
            b = kk * a_ref[u].astype(F32)
            at = -kk * jnp.exp(cum - lw_ref[u])
            rt = r_ref[u].astype(F32) * jnp.exp(cum)
            pinv = jnp.exp(-cum)
            to_end = jnp.exp(cum[c - 1:c, :] - cum)
            lhs = jnp.concatenate([_split_heads(at, head0), _split_heads(rt, head0)], axis=0).astype(BF16)
            pre.append(dict(
                u=u, lhs=lhs, v16=v_ref[u], p_end=jnp.exp(cum[c - 1:c, :]),
                kt=_twice(k * pinv).astype(BF16), bt=_twice(b * pinv).astype(BF16),
                ar=jnp.concatenate([at, rt], axis=0).astype(BF16),
                kb_end=jnp.concatenate([(k * to_end).astype(BF16), (b * to_end).astype(BF16)], axis=0),
                tinv=_split_heads(t_ref[u].astype(F32), head0).astype(BF16)))
        return pre

    def scores_k(pre):
        for q in pre:
            q["sk"] = _dot_nt(q["lhs"], q["kt"])

    def scores_b(pre):
        for q in pre:
            sb = _dot_nt(q["lhs"][2 * c:], q["bt"])
            ak = jnp.where(mk["strict"], q["sk"][:2 * c], 0.0).astype(BF16)
            q["akv"] = _merge_heads(_dot(ak, _twice(q["v16"])), head0)
            q["rkb"] = jnp.concatenate([jnp.where(mk["incl"], q["sk"][2 * c:], 0.0),
                                        jnp.where(mk["incl"], sb, 0.0)], axis=1).astype(BF16)

    def finish(pre):
        for q in pre:
            u = q["u"]
            ls = u[1]
            vu2 = jnp.concatenate([_twice(q["v16"]), _twice(q["u16"])], axis=0)
            y = q["zs_r"] + _merge_heads(_dot(q["rkb"], vu2), head0)
            mean = per_head(y) * (1.0 / hd)
            yc = y - mean
            var = per_head(yc * yc) * (1.0 / hd)
            yn = yc * lax.rsqrt(var + GN_EPS) * gnw_ref[:, ls] + gnb_ref[:, ls]
            bonus = per_head(r_ref[u].astype(F32) * k_ref[u].astype(F32) * rk_ref[:, ls]) * v_ref[u].astype(F32)
            o_ref[u] = ((yn + bonus) * g_ref[u].astype(F32)).astype(o_ref.dtype)

    s = [s_ref[pi] for pi in range(pairs)]
    cur = prepare(0)
    scores_k(cur)
    scores_b(cur)
    prev = None
    for j in range(chunks):
        nxt = prepare(j + 1) if j + 1 < chunks else None
        zs = [_dot_nt(q["ar"], s[pi].astype(BF16)) for pi, q in enumerate(cur)]
        if nxt is not None:
            scores_k(nxt)
        for pi, q in enumerate(cur):
            z16 = (zs[pi][:c] + q["akv"]).astype(BF16)
            q["u16"] = _merge_heads(_dot(q["tinv"], _twice(z16)), head0).astype(BF16)
            q["zs_r"] = zs[pi][c:]
        if nxt is not None:
            scores_b(nxt)
        for pi, q in enumerate(cur):
            vu = jnp.concatenate([q["v16"], q["u16"]], axis=0)
            s[pi] = s[pi] * q["p_end"] + jnp.where(mk["same"], _dot_tn(vu, q["kb_end"]), 0.0)
        if prev is not None:
            finish(prev)
        prev, cur = cur, nxt
    finish(prev)
    for pi in range(pairs):
        s_ref[pi] = s[pi]


def _rwkv_scan(r, lw, k, v, kk, a, g, gn_w, gn_b, r_k, batch, seq, tc=256, pairs=8, tc_inv=128):
    t, d = r.shape
    tc = _pick(seq, tc)
    tc_inv = _pick(t, tc_inv)
    pairs = min(pairs, d // LANES)
    w = pairs * LANES
    itile = pl.BlockSpec((tc_inv, w), lambda i, pi: (i, pi))
    tinv, cum = pl.pallas_call(
        functools.partial(_scan_inv_kernel, chunks=tc_inv // SCAN_CHUNK, pairs=pairs),
        grid=(t // tc_inv, d // w),
        in_specs=[itile] * 3,
        out_specs=[itile] * 2,
        out_shape=[jax.ShapeDtypeStruct((t, d), BF16), jax.ShapeDtypeStruct((t, d), F32)],
        compiler_params=_cparams("parallel", "parallel"),
        name="rwkv_inv",
    )(lw, kk, a)
    steps = seq // tc
    tile = pl.BlockSpec((tc, w), lambda bi, pi, ci: (bi * steps + ci, pi))
    vec = pl.BlockSpec((1, w), lambda bi, pi, ci: (0, pi))
    return pl.pallas_call(
        functools.partial(_scan_kernel, chunks=tc // SCAN_CHUNK, pairs=pairs),
        grid=(batch, d // w, steps),
        in_specs=[tile] * 9 + [vec] * 3,
        out_specs=tile,
        out_shape=jax.ShapeDtypeStruct((t, d), BF16),
        scratch_shapes=[pltpu.VMEM((pairs, LANES, LANES), F32)],
        compiler_params=_cparams("parallel", "parallel", "arbitrary"),
        name="rwkv_scan",
    )(r, cum, lw, k, v, kk, a, g, tinv, gn_w.reshape(1, d), gn_b.reshape(1, d), r_k.reshape(1, d))


SPAN = BLOCK * BRANCHES[-1][1]
ATT_HPS = 2
UNITS_PER_ITER = 4


def _attn_kernel(*refs):
    nbr = len(BRANCHES)
    q_refs = [refs[i * ATT_HPS:(i + 1) * ATT_HPS] for i in range(nbr)]
    base = nbr * ATT_HPS
    kc_refs, vc_refs, kp_refs, vp_refs, bias_refs = (refs[base + j * nbr: base + (j + 1) * nbr] for j in range(5))
    o_ref, acc_ref, m_ref, l_ref, stage_ref = refs[base + 5 * nbr:]
    blk = BLOCK
    n = pl.program_id(1)
    scale = ATT_HEAD ** -0.5
    kj = lax.broadcasted_iota(jnp.int32, (ATT_HPS * blk, 2 * blk), 1)
    widest = nbr - 1
    d_max = BRANCHES[widest][1]
    sub = 4
    assert d_max == sub * sub and BRANCHES[1][1] == sub and BRANCHES[0][1] == 1
    run = SPAN // sub

    staged = [q_refs[widest][e] for e in range(ATT_HPS)] + [kc_refs[widest], vc_refs[widest],
                                                           kp_refs[widest], vp_refs[widest]]
    for a, src in enumerate(staged):
        for r4 in range(sub):
            stage_ref[a, r4 * run:(r4 + 1) * run, :] = src[pl.ds(r4, run, stride=sub), :]

    def load_unit(i, d, u):
        if i == widest:
            r_lo = u % sub
            off = r_lo * run + u // sub
            rows_s = pl.ds(off, blk, stride=sub)
            q = jnp.concatenate([stage_ref[e, rows_s, :] for e in range(ATT_HPS)], axis=0)
            kcat = jnp.concatenate([stage_ref[ATT_HPS + 2, rows_s, :], stage_ref[ATT_HPS, rows_s, :]], axis=0)
            vcat = jnp.concatenate([stage_ref[ATT_HPS + 3, rows_s, :], stage_ref[ATT_HPS + 1, rows_s, :]], axis=0)
            return q, kcat, vcat, pl.ds(u, blk, stride=d), True
        nb = u // d
        start = nb * (blk * d) + (u - nb * d)
        rows = pl.ds(start, blk, stride=d)
        q = jnp.concatenate([q_refs[i][e][rows, :] for e in range(ATT_HPS)], axis=0)
        prow_in_cur = pl.ds(jnp.maximum(start - blk * d, 0), blk, stride=d)
        prow_in_prev = pl.ds(u - nb * d, blk, stride=d)
        first = nb == 0
        kprev = jnp.where(first, kp_refs[i][prow_in_prev, :], kc_refs[i][prow_in_cur, :])
        vprev = jnp.where(first, vp_refs[i][prow_in_prev, :], vc_refs[i][prow_in_cur, :])
        kcat = jnp.concatenate([kprev, kc_refs[i][rows, :]], axis=0)
        vcat = jnp.concatenate([vprev, vc_refs[i][rows, :]], axis=0)
        return q, kcat, vcat, rows, first

    for order, i in enumerate(reversed(range(nbr))):
        d = BRANCHES[i][1]

        def body(it, carry, i=i, d=d, init=order == 0):
            loaded = [load_unit(i, d, it * UNITS_PER_ITER + x) for x in range(UNITS_PER_ITER)]
            bias = bias_refs[i][...].reshape(ATT_HPS * blk, 2 * blk)
            ss = []
            for q, kcat, _, _, first in loaded:
                s = _dot_nt((q * scale).astype(BF16), kcat.astype(BF16)) + bias
                ss.append(jnp.where((n == 0) & first & (kj < blk), -jnp.inf, s))
            ms = [jnp.max(s, axis=-1, keepdims=True) for s in ss]
            ps = [jnp.exp(s - m) for s, m in zip(ss, ms)]
            ls = [jnp.sum(p, axis=-1, keepdims=True) for p in ps]
            os_ = [_dot(p.astype(BF16), ld[2].astype(BF16)) for p, ld in zip(ps, loaded)]
            for x in range(UNITS_PER_ITER):
                rows = loaded[x][3]
                for e in range(ATT_HPS):
                    hr = slice(e * blk, (e + 1) * blk)
                    m_new = jnp.broadcast_to(ms[x][hr], (blk, ATT_HEAD))
                    l_new = jnp.broadcast_to(ls[x][hr], (blk, ATT_HEAD))
                    o_new = os_[x][hr]
                    if init:
                        m_ref[e, rows, :] = m_new
                        l_ref[e, rows, :] = l_new
                        acc_ref[e, rows, :] = o_new
                    else:
                        m_old = m_ref[e, rows, :]
                        m2 = jnp.maximum(m_old, m_new)
                        w_old = jnp.exp(m_old - m2)
                        w_new = jnp.exp(m_new - m2)
                        m_ref[e, rows, :] = m2
                        l_ref[e, rows, :] = l_ref[e, rows, :] * w_old + l_new * w_new
                        acc_ref[e, rows, :] = acc_ref[e, rows, :] * w_old + o_new * w_new
            return carry

        lax.fori_loop(0, (SPAN // blk) // UNITS_PER_ITER, body, 0)

    for e in range(ATT_HPS):
        o_ref[:, e * ATT_HEAD:(e + 1) * ATT_HEAD] = (acc_ref[e] / l_ref[e]).astype(o_ref.dtype)


def _attention(q, kv, biases, batch, seq, heads, groups):
    t = q.shape[0]
    nsb = seq // SPAN
    per_kv = heads // groups
    steps = heads // ATT_HPS
    in_specs, args = [], []
    for i in range(N_BR):
        for e in range(ATT_HPS):
            in_specs.append(pl.BlockSpec((SPAN, ATT_HEAD),
                                         lambda b, n, hp, i=i, e=e: (b * nsb + n, i * heads + hp * ATT_HPS + e)))
            args.append(q)
    kvcol = lambda i, sel, hp: (i * 2 + sel) * groups + (hp * ATT_HPS) // per_kv
    for sel in range(2):
        for i in range(N_BR):
            in_specs.append(pl.BlockSpec((SPAN, ATT_HEAD),
                                         lambda b, n, hp, i=i, sel=sel: (b * nsb + n, kvcol(i, sel, hp))))
            args.append(kv)
    for sel in range(2):
        for i, (_, d) in enumerate(BRANCHES):
            rows = BLOCK * d
            in_specs.append(pl.BlockSpec(
                (rows, ATT_HEAD),
                lambda b, n, hp, i=i, sel=sel, rows=rows: (jnp.maximum((b * nsb + n) * (SPAN // rows) - 1, 0),
                                                            kvcol(i, sel, hp))))
            args.append(kv)
    for i in range(N_BR):
        in_specs.append(pl.BlockSpec((ATT_HPS, BLOCK, 2 * BLOCK), lambda b, n, hp: (hp, 0, 0)))
        args.append(biases[i])
    return pl.pallas_call(
        _attn_kernel,
        grid=(batch, nsb, steps),
        in_specs=in_specs,
        out_specs=pl.BlockSpec((SPAN, ATT_HPS * ATT_HEAD), lambda b, n, hp: (b * nsb + n, hp)),
        out_shape=jax.ShapeDtypeStruct((t, heads * ATT_HEAD), BF16),
        scratch_shapes=[pltpu.VMEM((ATT_HPS, SPAN, ATT_HEAD), F32)] * 3
        + [pltpu.VMEM((ATT_HPS + 4, SPAN, ATT_HEAD), F32)],
        compiler_params=_cparams("parallel", "parallel", "arbitrary"),
        name="dilated_attn",
    )(*args)


def _t5_bucket(distance):
    max_exact = REL_BUCKETS // 2
    dist = np.asarray(distance, dtype=np.int64)
    scaled = np.log(np.maximum(dist, max_exact) / max_exact) / np.log(REL_MAX_DIST / max_exact)
    large = np.minimum(max_exact + (scaled * (REL_BUCKETS - max_exact)).astype(np.int64), REL_BUCKETS - 1)
    return np.where(dist < max_exact, dist, large).astype(np.int32)


def _bias_table(rel_bias, window, dilation):
    band = window // dilation
    qi = np.arange(BLOCK)[:, None]
    kj = np.arange(2 * BLOCK)[None, :]
    rel = qi + BLOCK - kj
    bucket = _t5_bucket(np.clip(rel, 0, band) * dilation)
    bias = jnp.transpose(rel_bias[bucket], (2, 0, 1)).astype(F32)
    return jnp.where(((rel >= 0) & (rel <= band))[None], bias, -jnp.inf)


def _conv_ffn_layer(x, norm_g, w_up, conv_w, conv_b, w_down, layer, seq):
    h = _rmsnorm(x, norm_g)
    act = _ffn_up(h, w_up, conv_w, conv_b[:, None, :], layer, seq)
    return _matmul(act, w_down, F32, layer=layer, epilogue="residual", extra=(x,), tm=512, tn=512)


def _rwkv_layer(x, v_first, layer, norm_g, mu, w0, w1, w2, a0, a1, a2, vres_w, g1, g2, k_k, k_a, r_k,
                w_r, w_k, w_v, w_o, gn_w, gn_b, batch, seq):
    d = x.shape[1]
    xr, xw, xk, xv, xa, xg = _rms_mix(x, norm_g, mu, seq)
    r = _matmul(xr, w_r, BF16, layer=layer)
    k = _matmul(xk, w_k, BF16, layer=layer)
    v = _matmul(xv, w_v, BF16, layer=layer)
    hw = _matmul(xw, w1, BF16, layer=layer, epilogue="tanh")
    ha = _matmul(xa, a1, BF16, layer=layer)
    hg = _matmul(xg, g1, BF16, layer=layer, epilogue="sigmoid")
    zeros = jnp.zeros((d,), F32)
    if vres_w is None:
        vec = jnp.stack([w0, a0, k_k, k_a, zeros, zeros, zeros, zeros])
        vres = None
    else:
        v0, v1, v2 = vres_w
        vec = jnp.stack([w0, a0, k_k, k_a, v0, zeros, zeros, zeros])
        vres = (_matmul(xv, v1, BF16), v2.astype(BF16), v_first)
    lw, k2, v2_, kk, a, g = _rwkv_prep(k, v, hw, ha, hg, w2.astype(BF16), a2.astype(BF16), g2.astype(BF16),
                                       vec, vres)
    if vres_w is None:
        v_first = v2_
    out = _rwkv_scan(r, lw, k2, v2_, kk, a, g, gn_w, gn_b, r_k, batch, seq)
    return _matmul(out, w_o, F32, layer=layer, epilogue="residual", extra=(x,)), v_first


def _attn_layer(x, h, kv, w_q, q_gain, w_o, j, biases, batch, seq, heads, groups):
    gain = jnp.broadcast_to(q_gain[:, None, :], (N_BR, heads, ATT_HEAD)).reshape(1, -1)
    q = _matmul(h, w_q, F32, layer=j, epilogue="headnorm", extra=(gain, jnp.ones_like(gain)))
    o = _attention(q, kv, biases, batch, seq, heads, groups)
    return _matmul(o, w_o, F32, layer=j, epilogue="residual", extra=(x,))


def kernel(x, norm_mix, norm_ffn, rwkv_mu, rwkv_w0, rwkv_w1, rwkv_w2, rwkv_a0, rwkv_a1, rwkv_a2, rwkv_v0, rwkv_v1, rwkv_v2, rwkv_g1, rwkv_g2, rwkv_k_k, rwkv_k_a, rwkv_r_k, rwkv_w_r, rwkv_w_k, rwkv_w_v, rwkv_w_o, rwkv_gn_w, rwkv_gn_b, norm_kv, attn_w_kv, attn_k_gain, attn_w_q, attn_q_gain, attn_w_o, rel_bias, ffn_w_up, ffn_conv_w, ffn_conv_b, ffn_w_down):
    batch, seq, d = x.shape
    depth = norm_mix.shape[0]
    n_a = rwkv_mu.shape[0]
    heads = attn_w_o.shape[-2] // ATT_HEAD
    groups = attn_w_kv.shape[-1] // (N_BR * 2 * ATT_HEAD)
    assert seq % SPAN == 0 and d % LANES == 0 and heads % groups == 0 and (heads // groups) % ATT_HPS == 0
    x = x.reshape(batch * seq, d)
    v_first = None
    kv = None
    biases = None
    w_down16 = ffn_w_down.astype(BF16)
    for layer in range(depth):
        if layer < n_a:
            vres_w = None if layer == 0 else (rwkv_v0[layer - 1], rwkv_v1[layer - 1], rwkv_v2[layer - 1])
            x, v_first = _rwkv_layer(
                x, v_first, layer, norm_mix[layer], rwkv_mu[layer], rwkv_w0[layer], rwkv_w1, rwkv_w2[layer],
                rwkv_a0[layer], rwkv_a1, rwkv_a2[layer], vres_w, rwkv_g1, rwkv_g2[layer], rwkv_k_k[layer],
                rwkv_k_a[layer], rwkv_r_k[layer], rwkv_w_r, rwkv_w_k, rwkv_w_v, rwkv_w_o, rwkv_gn_w[layer],
                rwkv_gn_b[layer], batch, seq)
        else:
            if layer == n_a:
                k_gain = jnp.broadcast_to(attn_k_gain[:, None, None, :], (N_BR, 2, groups, ATT_HEAD))
                k_flag = jnp.broadcast_to(jnp.array([1.0, 0.0], F32)[None, :, None, None],
                                          (N_BR, 2, groups, ATT_HEAD))
                kv = _matmul(_rmsnorm(x, norm_kv), attn_w_kv, F32, epilogue="headnorm",
                             extra=(k_gain.reshape(1, -1), k_flag.reshape(1, -1)))
                biases = [_bias_table(rel_bias, w, dl) for w, dl in BRANCHES]
            j = layer - n_a
            h = _rmsnorm(x, norm_mix[layer])
            x = _attn_layer(x, h, kv, attn_w_q, attn_q_gain[j], attn_w_o, j, biases, batch, seq, heads, groups)
        x = _conv_ffn_layer(x, norm_ffn[layer], ffn_w_up, ffn_conv_w, ffn_conv_b, w_down16, layer, seq)
    return x.reshape(batch, seq, d)
```
